```python
import jax, jax.numpy as jnp
from jax import lax
import numpy as np

D_MODEL = 2048
BATCH = 16
SEQ = 256
DEPTH = 2
DEC_BATCH = 4
DEC_SEQ = 4096
PAST_LEN = 512

GRID_W = 64
N_MIXERS = 2
N_ATTN_LAYERS = (DEPTH + 1) // 2
N_CONV_LAYERS = DEPTH // 2
N_HEADS = 32
N_KV_HEADS = 8
HEAD_DIM = 64
GROUP = N_HEADS // N_KV_HEADS
WINDOW = 128
BLOCK = 128
ROPE_BASE = 10000.0
CONV_W = 3
N_EXPERTS = 32
TOP_K = 4
D_FF = D_MODEL
MOE_BLOCK = 128
SWIGLU_ALPHA = 1.702
SWIGLU_LIMIT = 7.0
RMS_EPS = 1e-6
NEG = -1e30

kernel_name = "hybrid_swa_shortconv_moe_diffusion_step"


def _rmsnorm(x, g):
    xf = x.astype(jnp.float32)
    y = xf * lax.rsqrt(jnp.mean(xf * xf, axis=-1, keepdims=True) + RMS_EPS)
    return (y * g.astype(jnp.float32)).astype(x.dtype)


def _modulation(cond, w_ada, b_ada):
    m = jax.nn.silu(cond) @ w_ada + b_ada
    return [t[:, None, :] for t in jnp.split(m, 6, axis=-1)]


def _modulate(x, g, shift, scale):
    return _rmsnorm(x, g) * (1 + scale) + shift


def _qkv(h, w_qkv):
    B, S, _ = h.shape
    qkv = h @ w_qkv
    q = qkv[..., : N_HEADS * HEAD_DIM].reshape(B, S, N_HEADS, HEAD_DIM)
    k = qkv[..., N_HEADS * HEAD_DIM:(N_HEADS + N_KV_HEADS) * HEAD_DIM].reshape(B, S, N_KV_HEADS, HEAD_DIM)
    v = qkv[..., (N_HEADS + N_KV_HEADS) * HEAD_DIM:].reshape(B, S, N_KV_HEADS, HEAD_DIM)
    return q, k, v


def _rope_axis(x, pos):
    half = x.shape[-1] // 2
    inv = ROPE_BASE ** (-jnp.arange(half, dtype=jnp.float32) / half)
    ang = pos.astype(jnp.float32)[:, None] * inv
    cos = jnp.cos(ang)[:, None, :]
    sin = jnp.sin(ang)[:, None, :]
    xf = x.astype(jnp.float32)
    x1, x2 = xf[..., :half], xf[..., half:]
    return jnp.concatenate([x1 * cos - x2 * sin, x2 * cos + x1 * sin], axis=-1).astype(x.dtype)


def _rope2d(x, row, col):
    hd = x.shape[-1] // 2
    return jnp.concatenate([_rope_axis(x[..., :hd], row), _rope_axis(x[..., hd:], col)], axis=-1)


def _attn_block(q, sinks, kv_sets):
    B, Q = q.shape[0], q.shape[1]
    scale = HEAD_DIM ** -0.5
    sink = jnp.broadcast_to(sinks.astype(jnp.float32).reshape(N_KV_HEADS, GROUP)[None, :, :, None, None],
                            (B, N_KV_HEADS, GROUP, Q, 1))
    logits = [sink]
    for k, v, m in kv_sets:
        s = jnp.einsum('bqkgd,blkd->bkgql', q, k).astype(jnp.float32) * scale
        if m is not None:
            s = jnp.where(m, s, NEG)
        logits.append(s)
    p = jax.nn.softmax(jnp.concatenate(logits, axis=-1), axis=-1)
    out = None
    off = 1
    for k, v, m in kv_sets:
        L = k.shape[1]
        o = jnp.einsum('bkgql,blkd->bqkgd', p[..., off:off + L].astype(v.dtype), v)
        out = o if out is None else out + o
        off += L
    return out


def _context_attention(h, w_qkv, w_o, sinks):
    B, S, _ = h.shape
    q, k, v = _qkv(h, w_qkv)
    nc = S // BLOCK
    qb = jnp.moveaxis(q.reshape(B, nc, BLOCK, N_KV_HEADS, GROUP, HEAD_DIM), 1, 0)
    out = lax.map(lambda qblk: _attn_block(qblk, sinks, [(k, v, None)]), qb)
    out = jnp.moveaxis(out, 0, 1).reshape(B, S, N_HEADS * HEAD_DIM)
    return out @ w_o, k, v


def _band(t, nb):
    B = t.shape[0]
    tp = jnp.pad(t, ((0, 0), (BLOCK, BLOCK), (0, 0), (0, 0)))
    tb = tp.reshape(B, nb + 2, BLOCK, N_KV_HEADS, HEAD_DIM)
    tw = jnp.concatenate([tb[:, :-2], tb[:, 1:-1], tb[:, 2:]], axis=2)
    return jnp.moveaxis(tw, 1, 0)


def _latent_attention(h, ck, cv, w_qkv, w_o, sinks):
    B, S, _ = h.shape
    rows = S // GRID_W
    row = jnp.repeat(jnp.arange(rows, dtype=jnp.int32), GRID_W)
    col = jnp.tile(jnp.arange(GRID_W, dtype=jnp.int32), rows)
    q, k, v = _qkv(h, w_qkv)
    q = _rope2d(q, row, col)
    k = _rope2d(k, row, col)
    nb = S // BLOCK
    qb = jnp.moveaxis(q.reshape(B, nb, BLOCK, N_KV_HEADS, GROUP, HEAD_DIM), 1, 0)
    kw = _band(k, nb)
    vw = _band(v, nb)
    b_idx = jnp.arange(nb)[:, None, None]
    qpos = b_idx * BLOCK + jnp.arange(BLOCK)[None, :, None]
    kpos = (b_idx - 1) * BLOCK + jnp.arange(3 * BLOCK)[None, None, :]
    mask = (kpos >= 0) & (kpos < S) & (jnp.abs(qpos - kpos) <= WINDOW)
    out = lax.map(lambda a: _attn_block(a[0], sinks, [(a[1], a[2], a[3]), (ck, cv, None)]),
                  (qb, kw, vw, mask))
    out = jnp.moveaxis(out, 0, 1).reshape(B, S, N_HEADS * HEAD_DIM)
    return out @ w_o


def _short_conv(h, w_in, conv_w, w_out):
    bg, cg, xin = jnp.split(h @ w_in, 3, axis=-1)
    u = cg * xin
    u = lax.conv_general_dilated(u, conv_w[:, None, :].astype(u.dtype), window_strides=(1,),
                                 padding=[(CONV_W // 2, CONV_W // 2)],
                                 dimension_numbers=('NWC', 'WIO', 'NWC'), feature_group_count=D_MODEL)
    return (bg * u) @ w_out


def _moe(x, router_w, router_b, w_gu, b_gu, w_dn, b_dn):
    shp = x.shape
    xt = x.reshape(-1, D_MODEL)
    N = xt.shape[0]
    logits = (xt @ router_w + router_b).astype(jnp.float32)
    top_v, top_e = lax.top_k(logits, TOP_K)
    gates = jax.nn.softmax(top_v, axis=-1)
    A = N * TOP_K
    e_flat = top_e.reshape(A)
    tok_flat = jnp.repeat(jnp.arange(N, dtype=jnp.int32), TOP_K)
    g_flat = gates.reshape(A)
    order = jnp.argsort(e_flat)
    e_sorted = e_flat[order]
    counts = jnp.bincount(e_flat, length=N_EXPERTS)
    padded = (counts + MOE_BLOCK - 1) // MOE_BLOCK * MOE_BLOCK
    pad_end = jnp.cumsum(padded)
    pad_start = pad_end - padded
    grp_start = jnp.cumsum(counts) - counts
    dest = pad_start[e_sorted] + (jnp.arange(A) - grp_start[e_sorted])
    P = ((A + MOE_BLOCK - 1) // MOE_BLOCK) * MOE_BLOCK + N_EXPERTS * MOE_BLOCK
    nblk = P // MOE_BLOCK
    row_tok = jnp.full((P,), N, jnp.int32).at[dest].set(tok_flat[order])
    row_gate = jnp.zeros((P,), jnp.float32).at[dest].set(g_flat[order])
    blk_exp = jnp.minimum(jnp.searchsorted(pad_end, jnp.arange(nblk) * MOE_BLOCK, side='right'),
                          N_EXPERTS - 1)
    xpad = jnp.concatenate([xt, jnp.zeros((1, D_MODEL), xt.dtype)], axis=0)
    xb = xpad[row_tok].reshape(nblk, MOE_BLOCK, D_MODEL)

    def expert_block(args):
        xblk, e = args
        gu = xblk @ w_gu[e] + b_gu[e]
        gate = jnp.minimum(gu[:, :D_FF], SWIGLU_LIMIT)
        up = jnp.clip(gu[:, D_FF:], -SWIGLU_LIMIT, SWIGLU_LIMIT)
        hid = (up + 1) * (gate * jax.nn.sigmoid(SWIGLU_ALPHA * gate))
        return hid @ w_dn[e] + b_dn[e]

    yb = lax.map(expert_block, (xb, blk_exp)).reshape(P, D_MODEL)
    yb = yb * row_gate[:, None].astype(yb.dtype)
    out = jax.ops.segment_sum(yb, row_tok, num_segments=N + 1)[:N]
    return out.reshape(shp)


def setup_inputs(seed: int = 0) -> dict:
    key = jax.random.key(seed)
    ks = jax.random.split(key, 24)
    f32 = jnp.float32
    nrm = lambda k, s, sc: jax.random.normal(k, s, f32) * sc
    kvd = N_KV_HEADS * HEAD_DIM
    return {
        "x_prompt": nrm(ks[0], (BATCH, SEQ, D_MODEL), 1.0),
        "x_sample": nrm(ks[1], (DEC_BATCH, DEC_SEQ, D_MODEL), 1.0),
        "cache_k": nrm(ks[2], (DEC_BATCH, N_ATTN_LAYERS, PAST_LEN, N_KV_HEADS, HEAD_DIM), 1.0),
        "cache_v": nrm(ks[3], (DEC_BATCH, N_ATTN_LAYERS, PAST_LEN, N_KV_HEADS, HEAD_DIM), 1.0),
        "c": nrm(ks[4], (DEC_BATCH, D_MODEL), 1.0),
        "c_ctx": nrm(ks[5], (D_MODEL,), 1.0),
        "norm_g": 1.0 + nrm(ks[6], (DEPTH, 4, D_MODEL), 0.1),
        "w_ada": nrm(ks[7], (DEPTH, D_MODEL, 6 * D_MODEL), 0.5 * D_MODEL ** -0.5),
        "b_ada": nrm(ks[8], (DEPTH, 6 * D_MODEL), 0.02),
        "w_qkv": nrm(ks[9], (N_ATTN_LAYERS, D_MODEL, N_HEADS * HEAD_DIM + 2 * kvd), D_MODEL ** -0.5),
        "w_attn_o": nrm(ks[10], (N_ATTN_LAYERS, N_HEADS * HEAD_DIM, D_MODEL), (N_HEADS * HEAD_DIM) ** -0.5),
        "attn_sinks": nrm(ks[11], (N_ATTN_LAYERS, N_HEADS), 1.0),
        "w_conv_in": nrm(ks[12], (N_CONV_LAYERS, D_MODEL, 3 * D_MODEL), D_MODEL ** -0.5),
        "conv_w": nrm(ks[13], (N_CONV_LAYERS, CONV_W, D_MODEL), CONV_W ** -0.5),
        "w_conv_out": nrm(ks[14], (N_CONV_LAYERS, D_MODEL, D_MODEL), D_MODEL ** -0.5),
        "router_w": nrm(ks[15], (DEPTH, D_MODEL, N_EXPERTS), D_MODEL ** -0.5),
        "router_b": nrm(ks[16], (DEPTH, N_EXPERTS), 0.01),
        "w_gate_up": nrm(ks[17], (DEPTH, N_EXPERTS, D_MODEL, 2 * D_FF), D_MODEL ** -0.5),
        "b_gate_up": nrm(ks[18], (DEPTH, N_EXPERTS, 2 * D_FF), 0.02),
        "w_down": nrm(ks[19], (DEPTH, N_EXPERTS, D_FF, D_MODEL), D_FF ** -0.5),
        "b_down": nrm(ks[20], (DEPTH, N_EXPERTS, D_MODEL), 0.02),
    }


def reference(x_prompt, x_sample, cache_k, cache_v, c, c_ctx, norm_g, w_ada, b_ada, w_qkv, w_attn_o,
              attn_sinks, w_conv_in, conv_w, w_conv_out, router_w, router_b, w_gate_up, b_gate_up,
              w_down, b_down):
    yp = x_prompt
    ys = x_sample
    cond_ctx = c_ctx[None, :]
    new_k = []
    new_v = []
    for l in range(DEPTH):
        mp = _modulation(cond_ctx, w_ada[l], b_ada[l])
        ms = _modulation(c, w_ada[l], b_ada[l])
        g = norm_g[l]
        hp = _modulate(yp, g[0], mp[0], mp[1])
        hs = _modulate(ys, g[0], ms[0], ms[1])
        if l % N_MIXERS == 0:
            a = l // N_MIXERS
            op, kp, vp = _context_attention(hp, w_qkv[a], w_attn_o[a], attn_sinks[a])
            new_k.append(kp)
            new_v.append(vp)
            os_ = _latent_attention(hs, cache_k[:, a], cache_v[:, a], w_qkv[a], w_attn_o[a], attn_sinks[a])
        else:
            ci = l // N_MIXERS
            op = _short_conv(hp, w_conv_in[ci], conv_w[ci], w_conv_out[ci])
            os_ = _short_conv(hs, w_conv_in[ci], conv_w[ci], w_conv_out[ci])
        yp = yp + mp[2] * _rmsnorm(op, g[1])
        ys = ys + ms[2] * _rmsnorm(os_, g[1])
        hp = _modulate(yp, g[2], mp[3], mp[4])
        hs = _modulate(ys, g[2], ms[3], ms[4])
        fp = _moe(hp, router_w[l], router_b[l], w_gate_up[l], b_gate_up[l], w_down[l], b_down[l])
        fs = _moe(hs, router_w[l], router_b[l], w_gate_up[l], b_gate_up[l], w_down[l], b_down[l])
        yp = yp + mp[5] * _rmsnorm(fp, g[3])
        ys = ys + ms[5] * _rmsnorm(fs, g[3])
    new_cache_k = jnp.stack(new_k, axis=1)
    new_cache_v = jnp.stack(new_v, axis=1)
    return (yp, ys, new_cache_k, new_cache_v)
```

```python
import functools

import jax
import jax.numpy as jnp
from jax import lax
from jax.experimental import pallas as pl
from jax.experimental.pallas import tpu as pltpu

N_HEADS = 32
N_KV_HEADS = 8
HEAD_DIM = 64
GROUP = N_HEADS // N_KV_HEADS
SLAB = GROUP * HEAD_DIM
ATT_BLOCK = 128
GRID_W = 64
ROPE_BASE = 10000.0
N_EXPERTS = 32
TOP_K = 4
SWIGLU_ALPHA = 1.702
SWIGLU_LIMIT = 7.0
RMS_EPS = 1e-6
NEG = -1e30
LANES = 128
MOD_ROWS = 8

F32 = jnp.float32
BF16 = jnp.bfloat16

VMEM_LIMIT = 56 * 1024 * 1024


def _cparams(sem):
    return pltpu.CompilerParams(dimension_semantics=sem, vmem_limit_bytes=VMEM_LIMIT)


def _rms(x, g):
    return x * lax.rsqrt(jnp.mean(x * x, axis=-1, keepdims=True) + RMS_EPS) * g


def _mod_row(i, tm, n_ctx, dec_seq):
    r = i * tm
    return jnp.where(r < n_ctx, 0, 1 + (r - n_ctx) // dec_seq)


def _mod_kernel(cond_ref, w_ref, b_ref, o_ref):
    c = cond_ref[...]
    s = c * jax.nn.sigmoid(c)
    o_ref[...] = jnp.dot(s, w_ref[...], preferred_element_type=F32,
                         precision=lax.Precision.HIGHEST) + b_ref[...]


def _modulation(cond, w_ada, b_ada):
    L, D, D6 = w_ada.shape
    tn = 1024 if D6 % 1024 == 0 else D6
    out = pl.pallas_call(
        _mod_kernel,
        grid=(L, D6 // tn),
        in_specs=[pl.BlockSpec((MOD_ROWS, D), lambda l, j: (0, 0)),
                  pl.BlockSpec((None, D, tn), lambda l, j: (l, 0, j)),
                  pl.BlockSpec((None, 1, tn), lambda l, j: (l, 0, j))],
        out_specs=pl.BlockSpec((None, MOD_ROWS, tn), lambda l, j: (l, 0, j)),
        out_shape=jax.ShapeDtypeStruct((L, MOD_ROWS, D6), F32),
        name="adaln_mod",
        compiler_params=_cparams(("arbitrary", "arbitrary")),
    )(cond, w_ada, b_ada.reshape(L, 1, D6))
    return out.reshape(L, MOD_ROWS, 6, D)


def _qkv_kernel(x_ref, g_ref, mod_ref, w_ref, c_ref, s1_ref, s2_ref,
                q_ref, kh_ref, vh_ref, k32_ref, v32_ref, *, tm, n_ctx):
    i = pl.program_id(0)
    h = _rms(x_ref[...], g_ref[...]) * (1 + mod_ref[1:2, :]) + mod_ref[0:1, :]
    qkv = jnp.dot(h.astype(BF16), w_ref[...], preferred_element_type=F32)
    is_latent = i * tm >= n_ctx
    cos, sin_lo, sin_hi = c_ref[...], s1_ref[...], s2_ref[...]

    def rope(ch):
        r = ch * cos + pltpu.roll(ch, LANES - 16, 1) * sin_lo + pltpu.roll(ch, 16, 1) * sin_hi
        return jnp.where(is_latent, r, ch)

    nq = N_HEADS * HEAD_DIM
    nkv = N_KV_HEADS * HEAD_DIM
    for kh in range(N_KV_HEADS):
        parts = [rope(qkv[:, kh * SLAB + c * LANES: kh * SLAB + (c + 1) * LANES])
                 for c in range(SLAB // LANES)]
        q_ref[kh] = jnp.concatenate(parts, axis=1).astype(BF16)
    for c in range(nkv // LANES):
        kc = rope(qkv[:, nq + c * LANES: nq + (c + 1) * LANES])
        vc = qkv[:, nq + nkv + c * LANES: nq + nkv + (c + 1) * LANES]
        k32_ref[:, c * LANES:(c + 1) * LANES] = kc
        v32_ref[:, c * LANES:(c + 1) * LANES] = vc
        for s in range(LANES // HEAD_DIM):
            kh_ref[2 * c + s] = kc[:, s * HEAD_DIM:(s + 1) * HEAD_DIM].astype(BF16)
            vh_ref[2 * c + s] = vc[:, s * HEAD_DIM:(s + 1) * HEAD_DIM].astype(BF16)


def _qkv_proj(x, g, mod_l, w_bf, rope_tabs, *, tm, n_ctx, dec_seq):
    N, D = x.shape
    nq = N_HEADS * HEAD_DIM
    nkv = N_KV_HEADS * HEAD_DIM
    cos, sin_lo, sin_hi = rope_tabs
    tiles_per_seq = dec_seq // tm

    def tab_idx(i):
        t = jnp.maximum(i * tm - n_ctx, 0) // tm
        return (t % tiles_per_seq, 0)

    row = lambda i: (i, 0)
    return pl.pallas_call(
        functools.partial(_qkv_kernel, tm=tm, n_ctx=n_ctx),
        grid=(N // tm,),
        in_specs=[pl.BlockSpec((tm, D), row),
                  pl.BlockSpec((1, D), lambda i: (0, 0)),
                  pl.BlockSpec((None, 6, D), lambda i: (_mod_row(i, tm, n_ctx, dec_seq), 0, 0)),
                  pl.BlockSpec((D, nq + 2 * nkv), lambda i: (0, 0)),
                  pl.BlockSpec((tm, LANES), tab_idx),
                  pl.BlockSpec((tm, LANES), tab_idx),
                  pl.BlockSpec((tm, LANES), tab_idx)],
        out_specs=[pl.BlockSpec((N_KV_HEADS, tm, SLAB), lambda i: (0, i, 0)),
                   pl.BlockSpec((N_KV_HEADS, tm, HEAD_DIM), lambda i: (0, i, 0)),
                   pl.BlockSpec((N_KV_HEADS, tm, HEAD_DIM), lambda i: (0, i, 0)),
                   pl.BlockSpec((tm, nkv), row),
                   pl.BlockSpec((tm, nkv), row)],
        out_shape=[jax.ShapeDtypeStruct((N_KV_HEADS, N, SLAB), BF16),
                   jax.ShapeDtypeStruct((N_KV_HEADS, N, HEAD_DIM), BF16),
                   jax.ShapeDtypeStruct((N_KV_HEADS, N, HEAD_DIM), BF16),
                   jax.ShapeDtypeStruct((N, nkv), F32),
                   jax.ShapeDtypeStruct((N, nkv), F32)],
        name="qkv_proj",
        compiler_params=_cparams(("arbitrary",)),
    )(x, g, mod_l, w_bf, cos, sin_lo, sin_hi)


def _rope_tables(dec_seq):
    half = HEAD_DIM // 4
    inv = ROPE_BASE ** (-jnp.arange(half, dtype=F32) / half)
    pos = jnp.arange(dec_seq, dtype=jnp.int32)
    row = (pos // GRID_W).astype(F32)[:, None] * inv
    col = (pos % GRID_W).astype(F32)[:, None] * inv
    z = jnp.zeros_like(row)
    cos = jnp.concatenate([jnp.cos(row), jnp.cos(row), jnp.cos(col), jnp.cos(col)], axis=1)
    sin_lo = jnp.concatenate([-jnp.sin(row), z, -jnp.sin(col), z], axis=1)
    sin_hi = jnp.concatenate([z, jnp.sin(row), z, jnp.sin(col)], axis=1)
    rep = LANES // HEAD_DIM
    return tuple(jnp.tile(t, (1, rep)) for t in (cos, sin_lo, sin_hi))


def _attn_heads(q, k, v, mask, sink_ref, kh):
    T = q.shape[0]
    qs = jnp.concatenate([q[:, g * HEAD_DIM:(g + 1) * HEAD_DIM] for g in range(GROUP)], axis=0)
    s = lax.dot_general(qs, k, (((1,), (1,)), ((), ())), preferred_element_type=F32)
    if mask is not None:
        s = jnp.where(mask, s, NEG)
    head = lax.broadcasted_iota(jnp.int32, (GROUP * T, 1), 0) // T
    sink = jnp.zeros((GROUP * T, 1), F32)
    for g in range(GROUP):
        sink = jnp.where(head == g, sink_ref[kh * GROUP + g], sink)
    m = jnp.maximum(jnp.max(s, axis=-1, keepdims=True), sink)
    p = jnp.exp(s - m)
    den = jnp.sum(p, axis=-1, keepdims=True) + jnp.exp(sink - m)
    o = jnp.dot(p.astype(BF16), v, preferred_element_type=F32) / den
    return jnp.concatenate([o[g * T:(g + 1) * T] for g in range(GROUP)], axis=1)


def _ctx_attn_kernel(sink_ref, q_ref, k_ref, v_ref, o_ref):
    def body(kh, carry):
        o_ref[kh] = _attn_heads(q_ref[kh], k_ref[kh], v_ref[kh], None, sink_ref, kh).astype(BF16)
        return carry

    lax.fori_loop(0, N_KV_HEADS, body, 0)


def _lat_attn_kernel(sink_ref, q_ref, kp_ref, kc_ref, kn_ref, vp_ref, vc_ref, vn_ref,
                     ck_ref, cv_ref, o_ref, *, nb, past):
    i = pl.program_id(1)
    T = ATT_BLOCK
    shape = (GROUP * T, 3 * T + past)
    a = lax.broadcasted_iota(jnp.int32, shape, 0) % T
    col = lax.broadcasted_iota(jnp.int32, shape, 1)
    seg = col // T
    c = col % T
    mask = (((seg == 0) & (c >= a) & (i > 0)) | (seg == 1)
            | ((seg == 2) & (c <= a) & (i < nb - 1)) | (seg >= 3))

    def body(kh, carry):
        k = jnp.concatenate([kp_ref[kh], kc_ref[kh], kn_ref[kh], ck_ref[kh]], axis=0)
        v = jnp.concatenate([vp_ref[kh], vc_ref[kh], vn_ref[kh], cv_ref[kh]], axis=0)
        o_ref[kh] = _attn_heads(q_ref[kh], k, v, mask, sink_ref, kh).astype(BF16)
        return carry

    lax.fori_loop(0, N_KV_HEADS, body, 0)


def _attention(q, kh, vh, ck, cv, sinks, *, batch, seq, dec_batch, dec_seq):
    N = q.shape[1]
    n_ctx = batch * seq
    past = ck.shape[2]
    smem = pl.BlockSpec(memory_space=pltpu.SMEM)
    o_ctx = pl.pallas_call(
        _ctx_attn_kernel,
        grid=(batch,),
        in_specs=[smem,
                  pl.BlockSpec((N_KV_HEADS, seq, SLAB), lambda n: (0, n, 0)),
                  pl.BlockSpec((N_KV_HEADS, seq, HEAD_DIM), lambda n: (0, n, 0)),
                  pl.BlockSpec((N_KV_HEADS, seq, HEAD_DIM), lambda n: (0, n, 0))],
        out_specs=pl.BlockSpec((N_KV_HEADS, seq, SLAB), lambda n: (0, n, 0)),
        out_shape=jax.ShapeDtypeStruct((N_KV_HEADS, n_ctx, SLAB), BF16),
        name="ctx_attn",
        compiler_params=_cparams(("arbitrary",)),
    )(sinks, q, kh, vh)

    T = ATT_BLOCK
    nb = dec_seq // T
    base = n_ctx // T
    cur = lambda b, i: (0, base + b * nb + i, 0)
    prv = lambda b, i: (0, base + b * nb + jnp.maximum(i - 1, 0), 0)
    nxt = lambda b, i: (0, base + b * nb + jnp.minimum(i + 1, nb - 1), 0)
    kv_spec = lambda f: pl.BlockSpec((N_KV_HEADS, T, HEAD_DIM), f)
    ctx_spec = pl.BlockSpec((None, N_KV_HEADS, past, HEAD_DIM), lambda b, i: (b, 0, 0, 0))
    o_lat = pl.pallas_call(
        functools.partial(_lat_attn_kernel, nb=nb, past=past),
        grid=(dec_batch, nb),
        in_specs=[smem,
                  pl.BlockSpec((N_KV_HEADS, T, SLAB), cur),
                  kv_spec(prv), kv_spec(cur), kv_spec(nxt),
                  kv_spec(prv), kv_spec(cur), kv_spec(nxt),
                  ctx_spec, ctx_spec],
        out_specs=pl.BlockSpec((N_KV_HEADS, T, SLAB), lambda b, i: (0, b * nb + i, 0)),
        out_shape=jax.ShapeDtypeStruct((N_KV_HEADS, N - n_ctx, SLAB), BF16),
        name="lat_attn",
        compiler_params=_cparams(("arbitrary", "arbitrary")),
    )(sinks, q, kh, kh, kh, vh, vh, vh, ck, cv)
    return o_ctx, o_lat


def _conv_in_kernel(x_ref, g_ref, mod_ref, wb_ref, wc_ref, wx_ref, bg_ref, u_ref, h_ref):
    @pl.when(pl.program_id(1) == 0)
    def _():
        h = _rms(x_ref[...], g_ref[...]) * (1 + mod_ref[1:2, :]) + mod_ref[0:1, :]
        h_ref[...] = h.astype(BF16)

    h = h_ref[...]
    bg_ref[...] = jnp.dot(h, wb_ref[...], preferred_element_type=F32)
    cg = jnp.dot(h, wc_ref[...], preferred_element_type=F32)
    xin = jnp.dot(h, wx_ref[...], preferred_element_type=F32)
    u_ref[...] = cg * xin


def _conv_in(x, g, mod_l, w_bf, *, tm, tc, n_ctx, dec_seq):
    N, D = x.shape
    nc = D // tc
    wspec = lambda k: pl.BlockSpec((D, tc), lambda i, j: (0, k * nc + j))
    return pl.pallas_call(
        _conv_in_kernel,
        grid=(N // tm, nc),
        in_specs=[pl.BlockSpec((tm, D), lambda i, j: (i, 0)),
                  pl.BlockSpec((1, D), lambda i, j: (0, 0)),
                  pl.BlockSpec((None, 6, D), lambda i, j: (_mod_row(i, tm, n_ctx, dec_seq), 0, 0)),
                  wspec(0), wspec(1), wspec(2)],
        out_specs=[pl.BlockSpec((tm, tc), lambda i, j: (i, j)),
                   pl.BlockSpec((tm, tc), lambda i, j: (i, j))],
        out_shape=[jax.ShapeDtypeStruct((N, D), F32), jax.ShapeDtypeStruct((N, D), F32)],
        scratch_shapes=[pltpu.VMEM((tm, D), BF16)],
        name="conv_in",
        compiler_params=_cparams(("arbitrary", "arbitrary")),
    )(x, g, mod_l, w_bf, w_bf, w_bf)


def _post_epilogue(proj, xres_ref, g_ref, mod_ref, rw_ref, rb_ref, y_ref, h_ref, ti_ref, tg_ref):
    y = xres_ref[...] + mod_ref[2:3, :] * _rms(proj, g_ref[1:2, :])
    y_ref[...] = y
    h = _rms(y, g_ref[2:3, :]) * (1 + mod_ref[4:5, :]) + mod_ref[3:4, :]
    h_ref[...] = h
    logits = jnp.dot(h, rw_ref[...], preferred_element_type=F32,
                     precision=lax.Precision.HIGHEST) + rb_ref[...]
    tm = logits.shape[0]
    lane_e = lax.broadcasted_iota(jnp.int32, logits.shape, 1)
    lane_o = lax.broadcasted_iota(jnp.int32, (tm, LANES), 1)
    top_i = jnp.zeros((tm, LANES), jnp.int32)
    top_p = jnp.zeros((tm, LANES), F32)
    rest = logits
    first = None
    den = None
    for k in range(TOP_K):
        m = jnp.max(rest, axis=-1, keepdims=True)
        idx = jnp.min(jnp.where(rest == m, lane_e, N_EXPERTS), axis=-1, keepdims=True)
        rest = jnp.where(lane_e == idx, -jnp.inf, rest)
        if k == 0:
            first = m
        p = jnp.exp(m - first)
        den = p if k == 0 else den + p
        top_i = jnp.where(lane_o == k, idx, top_i)
        top_p = jnp.where(lane_o == k, p, top_p)
    ti_ref[...] = top_i
    tg_ref[...] = top_p / den


def _attn_out_kernel(oc_ref, ol_ref, w_ref, xres_ref, g_ref, mod_ref, rw_ref, rb_ref,
                     y_ref, h_ref, ti_ref, tg_ref, *, tm, n_ctx):
    is_latent = pl.program_id(0) * tm >= n_ctx
    proj = None
    for kh in range(N_KV_HEADS):
        o = jnp.where(is_latent, ol_ref[kh], oc_ref[kh])
        part = jnp.dot(o, w_ref[kh], preferred_element_type=F32)
        proj = part if proj is None else proj + part
    _post_epilogue(proj, xres_ref, g_ref, mod_ref, rw_ref, rb_ref, y_ref, h_ref, ti_ref, tg_ref)


def _conv_out_kernel(bg_ref, u_ref, up_ref, un_ref, cw_ref, w_ref, xres_ref, g_ref, mod_ref,
                     rw_ref, rb_ref, y_ref, h_ref, ti_ref, tg_ref, *, tm, n_ctx, seq, dec_seq):
    i = pl.program_id(0)
    u = u_ref[...]
    row = lax.broadcasted_iota(jnp.int32, (tm, 1), 0)
    r = i * tm + row
    pos = jnp.where(r < n_ctx, r % seq, (r - n_ctx) % dec_seq)
    last = jnp.where(r < n_ctx, seq - 1, dec_seq - 1)
    prev = jnp.where(row == 0, up_ref[7:8, :], pltpu.roll(u, 1, 0))
    prev = jnp.where(pos == 0, 0.0, prev)
    nxt = jnp.where(row == tm - 1, un_ref[0:1, :], pltpu.roll(u, tm - 1, 0))
    nxt = jnp.where(pos == last, 0.0, nxt)
    conv = prev * cw_ref[0:1, :] + u * cw_ref[1:2, :] + nxt * cw_ref[2:3, :]
    z = (bg_ref[...] * conv).astype(BF16)
    proj = jnp.dot(z, w_ref[...], preferred_element_type=F32)
    _post_epilogue(proj, xres_ref, g_ref, mod_ref, rw_ref, rb_ref, y_ref, h_ref, ti_ref, tg_ref)


def _mixer_out(name, kernel, lead_args, lead_specs, xres, g4, mod_l, router_w, router_b, *, tm, n_ctx,
               dec_seq):
    N, D = xres.shape
    row = lambda i: (i, 0)
    const = lambda i: (0, 0)
    return pl.pallas_call(
        kernel,
        grid=(N // tm,),
        in_specs=lead_specs + [
            pl.BlockSpec((tm, D), row),
            pl.BlockSpec((4, D), const),
            pl.BlockSpec((None, 6, D), lambda i: (_mod_row(i, tm, n_ctx, dec_seq), 0, 0)),
            pl.BlockSpec((D, N_EXPERTS), const),
            pl.BlockSpec((1, N_EXPERTS), const)],
        out_specs=[pl.BlockSpec((tm, D), row), pl.BlockSpec((tm, D), row),
                   pl.BlockSpec((tm, LANES), row), pl.BlockSpec((tm, LANES), row)],
        out_shape=[jax.ShapeDtypeStruct((N, D), F32), jax.ShapeDtypeStruct((N, D), F32),
                   jax.ShapeDtypeStruct((N, LANES), jnp.int32),
                   jax.ShapeDtypeStruct((N, LANES), F32)],
        name=name,
        compiler_params=_cparams(("arbitrary",)),
    )(*lead_args, xres, g4, mod_l, router_w, router_b)


def _attn_out(o_ctx, o_lat, w_bf, xres, g4, mod_l, router_w, router_b, *, tm, n_ctx, dec_seq):
    D = xres.shape[1]
    tc = n_ctx // tm
    specs = [pl.BlockSpec((N_KV_HEADS, tm, SLAB), lambda i: (0, jnp.minimum(i, tc - 1), 0)),
             pl.BlockSpec((N_KV_HEADS, tm, SLAB), lambda i: (0, jnp.maximum(i - tc, 0), 0)),
             pl.BlockSpec((N_KV_HEADS, SLAB, D), lambda i: (0, 0, 0))]
    kern = functools.partial(_attn_out_kernel, tm=tm, n_ctx=n_ctx)
    return _mixer_out("attn_out", kern, [o_ctx, o_lat, w_bf], specs, xres, g4, mod_l, router_w, router_b,
                      tm=tm, n_ctx=n_ctx, dec_seq=dec_seq)


def _conv_out(bg, u, conv_w, w_bf, xres, g4, mod_l, router_w, router_b, *, tm, n_ctx, seq, dec_seq):
    N, D = xres.shape
    sub = 8
    nsub = N // sub
    specs = [pl.BlockSpec((tm, D), lambda i: (i, 0)),
             pl.BlockSpec((tm, D), lambda i: (i, 0)),
             pl.BlockSpec((sub, D), lambda i: (jnp.maximum(i * (tm // sub) - 1, 0), 0)),
             pl.BlockSpec((sub, D), lambda i: (jnp.minimum((i + 1) * (tm // sub), nsub - 1), 0)),
             pl.BlockSpec((3, D), lambda i: (0, 0)),
             pl.BlockSpec((D, D), lambda i: (0, 0))]
    kern = functools.partial(_conv_out_kernel, tm=tm, n_ctx=n_ctx, seq=seq, dec_seq=dec_seq)
    return _mixer_out("conv_out", kern, [bg, u, u, u, conv_w, w_bf], specs, xres, g4, mod_l, router_w, router_b,
                      tm=tm, n_ctx=n_ctx, dec_seq=dec_seq)


def _routing(top_e, tm):
    N = top_e.shape[0]
    A = N * TOP_K
    nblk = A // tm + N_EXPERTS
    e_flat = top_e.reshape(A)
    order = jnp.argsort(e_flat, stable=True)
    e_sorted = e_flat[order]
    counts = jnp.zeros((N_EXPERTS,), jnp.int32).at[e_flat].add(1)
    blocks = (counts + tm - 1) // tm
    blk_end = jnp.cumsum(blocks)
    pad_start = (blk_end - blocks) * tm
    grp_start = jnp.cumsum(counts) - counts
    dest = pad_start[e_sorted] + (jnp.arange(A, dtype=jnp.int32) - grp_start[e_sorted])
    slot = jnp.zeros((A,), jnp.int32).at[order].set(dest)
    row_tok = jnp.zeros((nblk * tm,), jnp.int32).at[dest].set((order // TOP_K).astype(jnp.int32))
    blk_exp = jnp.minimum(jnp.searchsorted(blk_end, jnp.arange(nblk, dtype=jnp.int32), side='right'),
                          N_EXPERTS - 1).astype(jnp.int32)
    n_used = blk_end[-1:].astype(jnp.int32)
    return slot, row_tok, blk_exp, n_used


def _row_copy(src_ref, dst_ref, s, d, sem):
    return pltpu.make_async_copy(src_ref.at[pl.ds(s, 1), :], dst_ref.at[pl.ds(d, 1), :], sem)


def _gather_kernel(tok_ref, src_ref, dst_ref, sem, *, rows):
    base = pl.program_id(0) * rows

    def start(r, c):
        _row_copy(src_ref, dst_ref, tok_ref[base + r], base + r, sem).start()
        return c

    def wait(r, c):
        _row_copy(src_ref, dst_ref, tok_ref[base + r], base + r, sem).wait()
        return c

    lax.fori_loop(0, rows, start, 0)
    lax.fori_loop(0, rows, wait, 0)


def _gather_rows(row_tok, src, *, rows):
    P = row_tok.shape[0]
    D = src.shape[1]
    return pl.pallas_call(
        functools.partial(_gather_kernel, rows=rows),
        grid_spec=pltpu.PrefetchScalarGridSpec(
            num_scalar_prefetch=1, grid=(P // rows,),
            in_specs=[pl.BlockSpec(memory_space=pl.ANY)],
            out_specs=pl.BlockSpec(memory_space=pl.ANY),
            scratch_shapes=[pltpu.SemaphoreType.DMA]),
        out_shape=jax.ShapeDtypeStruct((P, D), src.dtype),
        name="moe_dispatch",
        compiler_params=_cparams(("arbitrary",)),
    )(row_tok, src)


def _expert_kernel(be_ref, nu_ref, x_ref, wg_ref, wu_ref, bgt_ref, bup_ref, wd_ref, bd_ref,
                   y_ref, xb_ref):
    i = pl.program_id(0)
    f = pl.program_id(1)
    used = i < nu_ref[0]

    @pl.when(used & (f == 0))
    def _():
        xb_ref[...] = x_ref[...].astype(BF16)

    @pl.when(used)
    def _():
        xb = xb_ref[...]
        gate = jnp.dot(xb, wg_ref[...], preferred_element_type=F32) + bgt_ref[...]
        up = jnp.dot(xb, wu_ref[...], preferred_element_type=F32) + bup_ref[...]
        gate = jnp.minimum(gate, SWIGLU_LIMIT)
        up = jnp.clip(up, -SWIGLU_LIMIT, SWIGLU_LIMIT)
        hid = (up + 1) * (gate * jax.nn.sigmoid(SWIGLU_ALPHA * gate))
        part = jnp.dot(hid.astype(BF16), wd_ref[...], preferred_element_type=F32)

        @pl.when(f == 0)
        def _():
            y_ref[...] = part + bd_ref[...]

        @pl.when(f != 0)
        def _():
            y_ref[...] += part

    @pl.when(jnp.logical_not(used) & (f == 0))
    def _():
        y_ref[...] = jnp.zeros_like(y_ref)


def _experts(xs, blk_exp, n_used, w_gu, b_gu, w_dn, b_dn, *, tm, tf):
    P, D = xs.shape
    E, _, F2 = w_gu.shape
    FF = F2 // 2
    nf = FF // tf
    nblk = P // tm

    def blk(i, nu):
        return jnp.minimum(i, nu[0] - 1)

    def fidx(i, f, nu):
        return jnp.where(i < nu[0], f, nf - 1)

    def exp(i, be, nu):
        return be[blk(i, nu)]

    return pl.pallas_call(
        _expert_kernel,
        grid_spec=pltpu.PrefetchScalarGridSpec(
            num_scalar_prefetch=2, grid=(nblk, nf),
            in_specs=[
                pl.BlockSpec((tm, D), lambda i, f, be, nu: (blk(i, nu), 0)),
                pl.BlockSpec((None, D, tf), lambda i, f, be, nu: (exp(i, be, nu), 0, fidx(i, f, nu))),
                pl.BlockSpec((None, D, tf), lambda i, f, be, nu: (exp(i, be, nu), 0, nf + fidx(i, f, nu))),
                pl.BlockSpec((None, 1, tf), lambda i, f, be, nu: (exp(i, be, nu), 0, fidx(i, f, nu))),
                pl.BlockSpec((None, 1, tf), lambda i, f, be, nu: (exp(i, be, nu), 0, nf + fidx(i, f, nu))),
                pl.BlockSpec((None, tf, D), lambda i, f, be, nu: (exp(i, be, nu), fidx(i, f, nu), 0)),
                pl.BlockSpec((None, 1, D), lambda i, f, be, nu: (exp(i, be, nu), 0, 0))],
            out_specs=pl.BlockSpec((tm, D), lambda i, f, be, nu: (i, 0)),
            scratch_shapes=[pltpu.VMEM((tm, D), BF16)]),
        out_shape=jax.ShapeDtypeStruct((P, D), F32),
        name="moe_experts",
        compiler_params=_cparams(("arbitrary", "arbitrary")),
    )(blk_exp, n_used, xs, w_gu, w_gu, b_gu.reshape(E, 1, F2), b_gu.reshape(E, 1, F2),
      w_dn, b_dn.reshape(E, 1, D))


def _combine_kernel(slot_ref, y_ref, tg_ref, xres_ref, g_ref, mod_ref, o_ref, buf_ref, sem, *, tm):
    base = pl.program_id(0) * tm * TOP_K

    def start(r, c):
        for k in range(TOP_K):
            _row_copy(y_ref, buf_ref.at[k], slot_ref[base + r * TOP_K + k], r, sem).start()
        return c

    def wait(r, c):
        for k in range(TOP_K):
            _row_copy(y_ref, buf_ref.at[k], slot_ref[base + r * TOP_K + k], r, sem).wait()
        return c

    lax.fori_loop(0, tm, start, 0)
    lax.fori_loop(0, tm, wait, 0)
    tg = tg_ref[...]
    f = tg[:, 0:1] * buf_ref[0]
    for k in range(1, TOP_K):
        f = f + tg[:, k:k + 1] * buf_ref[k]
    o_ref[...] = xres_ref[...] + mod_ref[5:6, :] * _rms(f, g_ref[3:4, :])


def _combine(slot, y, top_g, xres, g4, mod_l, *, tm, n_ctx, dec_seq):
    N, D = xres.shape
    return pl.pallas_call(
        functools.partial(_combine_kernel, tm=tm),
        grid_spec=pltpu.PrefetchScalarGridSpec(
            num_scalar_prefetch=1, grid=(N // tm,),
            in_specs=[pl.BlockSpec(memory_space=pl.ANY),
                      pl.BlockSpec((tm, LANES), lambda i, s: (i, 0)),
                      pl.BlockSpec((tm, D), lambda i, s: (i, 0)),
                      pl.BlockSpec((4, D), lambda i, s: (0, 0)),
                      pl.BlockSpec((None, 6, D), lambda i, s: (_mod_row(i, tm, n_ctx, dec_seq), 0, 0))],
            out_specs=pl.BlockSpec((tm, D), lambda i, s: (i, 0)),
            scratch_shapes=[pltpu.VMEM((TOP_K, tm, D), F32), pltpu.SemaphoreType.DMA]),
        out_shape=jax.ShapeDtypeStruct((N, D), F32),
        name="moe_combine",
        compiler_params=_cparams(("arbitrary",)),
    )(slot, y, top_g, xres, g4, mod_l)


def _moe(h, top_i, top_g, xres, g4, mod_l, w_gu, b_gu, w_dn, b_dn, *, tm_e, tf, tm_c, n_ctx, dec_seq):
    slot, row_tok, blk_exp, n_used = _routing(top_i[:, :TOP_K], tm_e)
    xs = _gather_rows(row_tok, h, rows=tm_e)
    y = _experts(xs, blk_exp, n_used, w_gu, b_gu, w_dn, b_dn, tm=tm_e, tf=tf)
    return _combine(slot, y, top_g, xres, g4, mod_l, tm=tm_c, n_ctx=n_ctx, dec_seq=dec_seq)


def kernel(x_prompt, x_sample, cache_k, cache_v, c, c_ctx, norm_g, w_ada, b_ada, w_qkv, w_attn_o,
           attn_sinks, w_conv_in, conv_w, w_conv_out, router_w, router_b, w_gate_up, b_gate_up,
           w_down, b_down):
    batch, seq, D = x_prompt.shape
    dec_batch, dec_seq, _ = x_sample.shape
    depth = norm_g.shape[0]
    n_ctx = batch * seq
    n_lat = dec_batch * dec_seq
    N = n_ctx + n_lat
    nq = N_HEADS * HEAD_DIM
    tm = 256
    tm_e = 512
    tf = min(512, D)
    assert dec_batch + 1 <= MOD_ROWS and seq % tm == 0 and dec_seq % tm == 0 and n_ctx % tm_e == 0
    assert dec_seq % ATT_BLOCK == 0 and n_ctx % ATT_BLOCK == 0 and (N * TOP_K) % tm_e == 0

    x = jnp.concatenate([x_prompt.reshape(n_ctx, D), x_sample.reshape(n_lat, D)], axis=0)
    cond = jnp.zeros((MOD_ROWS, D), F32).at[0].set(c_ctx).at[1:1 + dec_batch].set(c)
    mod = _modulation(cond, w_ada, b_ada)
    rope_tabs = _rope_tables(dec_seq)
    kw = dict(n_ctx=n_ctx, dec_seq=dec_seq)

    new_k, new_v = [], []
    for l in range(depth):
        g4 = norm_g[l]
        mod_l = mod[l]
        if l % 2 == 0:
            a = l // 2
            w = jnp.concatenate([w_qkv[a][:, :nq] * (HEAD_DIM ** -0.5), w_qkv[a][:, nq:]], axis=1)
            q, kh, vh, k32, v32 = _qkv_proj(x, g4[0:1], mod_l, w.astype(BF16), rope_tabs, tm=tm, **kw)
            new_k.append(k32[:n_ctx].reshape(batch, seq, N_KV_HEADS, HEAD_DIM))
            new_v.append(v32[:n_ctx].reshape(batch, seq, N_KV_HEADS, HEAD_DIM))
            ck = jnp.transpose(cache_k[:, a], (0, 2, 1, 3)).astype(BF16)
            cv = jnp.transpose(cache_v[:, a], (0, 2, 1, 3)).astype(BF16)
            o_ctx, o_lat = _attention(q, kh, vh, ck, cv, attn_sinks[a], batch=batch, seq=seq,
                                      dec_batch=dec_batch, dec_seq=dec_seq)
            w_o = w_attn_o[a].astype(BF16).reshape(N_KV_HEADS, SLAB, D)
            x, h, top_i, top_g = _attn_out(o_ctx, o_lat, w_o, x, g4, mod_l, router_w[l], router_b[l][None],
                                           tm=tm, **kw)
        else:
            ci = l // 2
            bg, u = _conv_in(x, g4[0:1], mod_l, w_conv_in[ci].astype(BF16), tm=tm_e,
                             tc=min(512, D), **kw)
            x, h, top_i, top_g = _conv_out(bg, u, conv_w[ci], w_conv_out[ci].astype(BF16), x, g4,
                                           mod_l, router_w[l], router_b[l][None], tm=tm, seq=seq, **kw)
        x = _moe(h, top_i, top_g, x, g4, mod_l, w_gate_up[l].astype(BF16), b_gate_up[l],
                 w_down[l].astype(BF16), b_down[l], tm_e=tm_e, tf=tf, tm_c=tm, **kw)

    y_prompt = x[:n_ctx].reshape(batch, seq, D)
    y_sample = x[n_ctx:].reshape(dec_batch, dec_seq, D)
    return (y_prompt, y_sample, jnp.stack(new_k, axis=1), jnp.stack(new_v, axis=1))
```

```python
import functools

import jax
import jax.numpy as jnp
from jax import lax
from jax.experimental import pallas as pl
from jax.experimental.pallas import tpu as pltpu

N_HEADS = 32
N_KV_HEADS = 8
HEAD_DIM = 64
GROUP = N_HEADS // N_KV_HEADS
SLAB = GROUP * HEAD_DIM
ATT_BLOCK = 128
GRID_W = 64
ROPE_BASE = 10000.0
N_EXPERTS = 32
TOP_K = 4
SWIGLU_ALPHA = 1.702
SWIGLU_LIMIT = 7.0
RMS_EPS = 1e-6
NEG = -1e30
LANES = 128
MOD_ROWS = 8

F32 = jnp.float32
BF16 = jnp.bfloat16

VMEM_LIMIT = 56 * 1024 * 1024


def _cparams(sem):
    return pltpu.CompilerParams(dimension_semantics=sem, vmem_limit_bytes=VMEM_LIMIT)


def _rms(x, g):
    return x * lax.rsqrt(jnp.mean(x * x, axis=-1, keepdims=True) + RMS_EPS) * g


def _mod_row(i, tm, n_ctx, dec_seq):
    r = i * tm
    return jnp.where(r < n_ctx, 0, 1 + (r - n_ctx) // dec_seq)


def _mod_kernel(cond_ref, w_ref, b_ref, o_ref):
    c = cond_ref[...]
    s = c * jax.nn.sigmoid(c)
    o_ref[...] = jnp.dot(s, w_ref[...], preferred_element_type=F32,
                         precision=lax.Precision.HIGHEST) + b_ref[...]


def _modulation(cond, w_ada, b_ada):
    L, D, D6 = w_ada.shape
    tn = 1024 if D6 % 1024 == 0 else D6
    out = pl.pallas_call(
        _mod_kernel,
        grid=(L, D6 // tn),
        in_specs=[pl.BlockSpec((MOD_ROWS, D), lambda l, j: (0, 0)),
                  pl.BlockSpec((None, D, tn), lambda l, j: (l, 0, j)),
                  pl.BlockSpec((None, 1, tn), lambda l, j: (l, 0, j))],
        out_specs=pl.BlockSpec((None, MOD_ROWS, tn), lambda l, j: (l, 0, j)),
        out_shape=jax.ShapeDtypeStruct((L, MOD_ROWS, D6), F32),
        name="adaln_mod",
        compiler_params=_cparams(("arbitrary", "arbitrary")),
    )(cond, w_ada, b_ada.reshape(L, 1, D6))
    return out.reshape(L, MOD_ROWS, 6, D)


def _qkv_kernel(x_ref, g_ref, mod_ref, w_ref, c_ref, s1_ref, s2_ref,
                q_ref, kh_ref, vh_ref, k32_ref, v32_ref, *, tm, n_ctx):
    i = pl.program_id(0)
    h = _rms(x_ref[...], g_ref[...]) * (1 + mod_ref[1:2, :]) + mod_ref[0:1, :]
    qkv = jnp.dot(h.astype(BF16), w_ref[...], preferred_element_type=F32)
    is_latent = i * tm >= n_ctx
    cos, sin_lo, sin_hi = c_ref[...], s1_ref[...], s2_ref[...]

    def rope(ch):
        r = ch * cos + pltpu.roll(ch, LANES - 16, 1) * sin_lo + pltpu.roll(ch, 16, 1) * sin_hi
        return jnp.where(is_latent, r, ch)

    nq = N_HEADS * HEAD_DIM
    nkv = N_KV_HEADS * HEAD_DIM
    for kh in range(N_KV_HEADS):
        parts = [rope(qkv[:, kh * SLAB + c * LANES: kh * SLAB + (c + 1) * LANES])
                 for c in range(SLAB // LANES)]
        q_ref[kh] = jnp.concatenate(parts, axis=1).astype(BF16)
    for c in range(nkv // LANES):
        kc = rope(qkv[:, nq + c * LANES: nq + (c + 1) * LANES])
        vc = qkv[:, nq + nkv + c * LANES: nq + nkv + (c + 1) * LANES]
        k32_ref[:, c * LANES:(c + 1) * LANES] = kc
        v32_ref[:, c * LANES:(c + 1) * LANES] = vc
        for s in range(LANES // HEAD_DIM):
            kh_ref[2 * c + s] = kc[:, s * HEAD_DIM:(s + 1) * HEAD_DIM].astype(BF16)
            vh_ref[2 * c + s] = vc[:, s * HEAD_DIM:(s + 1) * HEAD_DIM].astype(BF16)


def _qkv_proj(x, g, mod_l, w_bf, rope_tabs, *, tm, n_ctx, dec_seq):
    N, D = x.shape
    nq = N_HEADS * HEAD_DIM
    nkv = N_KV_HEADS * HEAD_DIM
    cos, sin_lo, sin_hi = rope_tabs
    tiles_per_seq = dec_seq // tm

    def tab_idx(i):
        t = jnp.maximum(i * tm - n_ctx, 0) // tm
        return (t % tiles_per_seq, 0)

    row = lambda i: (i, 0)
    return pl.pallas_call(
        functools.partial(_qkv_kernel, tm=tm, n_ctx=n_ctx),
        grid=(N // tm,),
        in_specs=[pl.BlockSpec((tm, D), row),
                  pl.BlockSpec((1, D), lambda i: (0, 0)),
                  pl.BlockSpec((None, 6, D), lambda i: (_mod_row(i, tm, n_ctx, dec_seq), 0, 0)),
                  pl.BlockSpec((D, nq + 2 * nkv), lambda i: (0, 0)),
                  pl.BlockSpec((tm, LANES), tab_idx),
                  pl.BlockSpec((tm, LANES), tab_idx),
                  pl.BlockSpec((tm, LANES), tab_idx)],
        out_specs=[pl.BlockSpec((N_KV_HEADS, tm, SLAB), lambda i: (0, i, 0)),
                   pl.BlockSpec((N_KV_HEADS, tm, HEAD_DIM), lambda i: (0, i, 0)),
                   pl.BlockSpec((N_KV_HEADS, tm, HEAD_DIM), lambda i: (0, i, 0)),
                   pl.BlockSpec((tm, nkv), row),
                   pl.BlockSpec((tm, nkv), row)],
        out_shape=[jax.ShapeDtypeStruct((N_KV_HEADS, N, SLAB), BF16),
                   jax.ShapeDtypeStruct((N_KV_HEADS, N, HEAD_DIM), BF16),
                   jax.ShapeDtypeStruct((N_KV_HEADS, N, HEAD_DIM), BF16),
                   jax.ShapeDtypeStruct((N, nkv), F32),
                   jax.ShapeDtypeStruct((N, nkv), F32)],
        name="qkv_proj",
        compiler_params=_cparams(("arbitrary",)),
    )(x, g, mod_l, w_bf, cos, sin_lo, sin_hi)


def _rope_tables(dec_seq):
    half = HEAD_DIM // 4
    inv = ROPE_BASE ** (-jnp.arange(half, dtype=F32) / half)
    pos = jnp.arange(dec_seq, dtype=jnp.int32)
    row = (pos // GRID_W).astype(F32)[:, None] * inv
    col = (pos % GRID_W).astype(F32)[:, None] * inv
    z = jnp.zeros_like(row)
    cos = jnp.concatenate([jnp.cos(row), jnp.cos(row), jnp.cos(col), jnp.cos(col)], axis=1)
    sin_lo = jnp.concatenate([-jnp.sin(row), z, -jnp.sin(col), z], axis=1)
    sin_hi = jnp.concatenate([z, jnp.sin(row), z, jnp.sin(col)], axis=1)
    rep = LANES // HEAD_DIM
    return tuple(jnp.tile(t, (1, rep)) for t in (cos, sin_lo, sin_hi))


def _attn_heads(q, k, v, mask, sink_ref, kh):
    T = q.shape[0]
    qs = jnp.concatenate([q[:, g * HEAD_DIM:(g + 1) * HEAD_DIM] for g in range(GROUP)], axis=0)
    s = lax.dot_general(qs, k, (((1,), (1,)), ((), ())), preferred_element_type=F32)
    if mask is not None:
        s = jnp.where(mask, s, NEG)
    head = lax.broadcasted_iota(jnp.int32, (GROUP * T, 1), 0) // T
    sink = jnp.zeros((GROUP * T, 1), F32)
    for g in range(GROUP):
        sink = jnp.where(head == g, sink_ref[kh * GROUP + g], sink)
    m = jnp.maximum(jnp.max(s, axis=-1, keepdims=True), sink)
    p = jnp.exp(s - m)
    den = jnp.sum(p, axis=-1, keepdims=True) + jnp.exp(sink - m)
    o = jnp.dot(p.astype(BF16), v, preferred_element_type=F32) / den
    return jnp.concatenate([o[g * T:(g + 1) * T] for g in range(GROUP)], axis=1)


def _ctx_attn_kernel(sink_ref, q_ref, k_ref, v_ref, o_ref):
    def body(kh, carry):
        o_ref[kh] = _attn_heads(q_ref[kh], k_ref[kh], v_ref[kh], None, sink_ref, kh).astype(BF16)
        return carry

    lax.fori_loop(0, N_KV_HEADS, body, 0)


def _lat_attn_kernel(sink_ref, q_ref, kp_ref, kc_ref, kn_ref, vp_ref, vc_ref, vn_ref,
                     ck_ref, cv_ref, o_ref, *, nb, past):
    i = pl.program_id(1)
    T = ATT_BLOCK
    shape = (GROUP * T, 3 * T + past)
    a = lax.broadcasted_iota(jnp.int32, shape, 0) % T
    col = lax.broadcasted_iota(jnp.int32, shape, 1)
    seg = col // T
    c = col % T
    mask = (((seg == 0) & (c >= a) & (i > 0)) | (seg == 1)
            | ((seg == 2) & (c <= a) & (i < nb - 1)) | (seg >= 3))

    def body(kh, carry):
        k = jnp.concatenate([kp_ref[kh], kc_ref[kh], kn_ref[kh], ck_ref[kh]], axis=0)
        v = jnp.concatenate([vp_ref[kh], vc_ref[kh], vn_ref[kh], cv_ref[kh]], axis=0)
        o_ref[kh] = _attn_heads(q_ref[kh], k, v, mask, sink_ref, kh).astype(BF16)
        return carry

    lax.fori_loop(0, N_KV_HEADS, body, 0)


def _attention(q, kh, vh, ck, cv, sinks, *, batch, seq, dec_batch, dec_seq):
    N = q.shape[1]
    n_ctx = batch * seq
    past = ck.shape[2]
    smem = pl.BlockSpec(memory_space=pltpu.SMEM)
    o_ctx = pl.pallas_call(
        _ctx_attn_kernel,
        grid=(batch,),
        in_specs=[smem,
                  pl.BlockSpec((N_KV_HEADS, seq, SLAB), lambda n: (0, n, 0)),
                  pl.BlockSpec((N_KV_HEADS, seq, HEAD_DIM), lambda n: (0, n, 0)),
                  pl.BlockSpec((N_KV_HEADS, seq, HEAD_DIM), lambda n: (0, n, 0))],
        out_specs=pl.BlockSpec((N_KV_HEADS, seq, SLAB), lambda n: (0, n, 0)),
        out_shape=jax.ShapeDtypeStruct((N_KV_HEADS, n_ctx, SLAB), BF16),
        name="ctx_attn",
        compiler_params=_cparams(("arbitrary",)),
    )(sinks, q, kh, vh)

    T = ATT_BLOCK
    nb = dec_seq // T
    base = n_ctx // T
    cur = lambda b, i: (0, base + b * nb + i, 0)
    prv = lambda b, i: (0, base + b * nb + jnp.maximum(i - 1, 0), 0)
    nxt = lambda b, i: (0, base + b * nb + jnp.minimum(i + 1, nb - 1), 0)
    kv_spec = lambda f: pl.BlockSpec((N_KV_HEADS, T, HEAD_DIM), f)
    ctx_spec = pl.BlockSpec((None, N_KV_HEADS, past, HEAD_DIM), lambda b, i: (b, 0, 0, 0))
    o_lat = pl.pallas_call(
        functools.partial(_lat_attn_kernel, nb=nb, past=past),
        grid=(dec_batch, nb),
        in_specs=[smem,
                  pl.BlockSpec((N_KV_HEADS, T, SLAB), cur),
                  kv_spec(prv), kv_spec(cur), kv_spec(nxt),
                  kv_spec(prv), kv_spec(cur), kv_spec(nxt),
                  ctx_spec, ctx_spec],
        out_specs=pl.BlockSpec((N_KV_HEADS, T, SLAB), lambda b, i: (0, b * nb + i, 0)),
        out_shape=jax.ShapeDtypeStruct((N_KV_HEADS, N - n_ctx, SLAB), BF16),
        name="lat_attn",
        compiler_params=_cparams(("arbitrary", "arbitrary")),
    )(sinks, q, kh, kh, kh, vh, vh, vh, ck, cv)
    return o_ctx, o_lat


def _conv_in_kernel(x_ref, g_ref, mod_ref, wb_ref, wc_ref, wx_ref, bg_ref, u_ref, h_ref):
    @pl.when(pl.program_id(1) == 0)
    def _():
        h = _rms(x_ref[...], g_ref[...]) * (1 + mod_ref[1:2, :]) + mod_ref[0:1, :]
        h_ref[...] = h.astype(BF16)

    h = h_ref[...]
    bg_ref[...] = jnp.dot(h, wb_ref[...], preferred_element_type=F32)
    cg = jnp.dot(h, wc_ref[...], preferred_element_type=F32)
    xin = jnp.dot(h, wx_ref[...], preferred_element_type=F32)
    u_ref[...] = cg * xin


def _conv_in(x, g, mod_l, w_bf, *, tm, tc, n_ctx, dec_seq):
    N, D = x.shape
    nc = D // tc
    wspec = lambda k: pl.BlockSpec((D, tc), lambda i, j: (0, k * nc + j))
    return pl.pallas_call(
        _conv_in_kernel,
        grid=(N // tm, nc),
        in_specs=[pl.BlockSpec((tm, D), lambda i, j: (i, 0)),
                  pl.BlockSpec((1, D), lambda i, j: (0, 0)),
                  pl.BlockSpec((None, 6, D), lambda i, j: (_mod_row(i, tm, n_ctx, dec_seq), 0, 0)),
                  wspec(0), wspec(1), wspec(2)],
        out_specs=[pl.BlockSpec((tm, tc), lambda i, j: (i, j)),
                   pl.BlockSpec((tm, tc), lambda i, j: (i, j))],
        out_shape=[jax.ShapeDtypeStruct((N, D), F32), jax.ShapeDtypeStruct((N, D), F32)],
        scratch_shapes=[pltpu.VMEM((tm, D), BF16)],
        name="conv_in",
        compiler_params=_cparams(("arbitrary", "arbitrary")),
    )(x, g, mod_l, w_bf, w_bf, w_bf)


def _post_epilogue(proj, xres_ref, g_ref, mod_ref, rw_ref, rb_ref, y_ref, h_ref, ti_ref, tg_ref, cnt_ref):
    @pl.when(pl.program_id(0) == 0)
    def _():
        cnt_ref[...] = jnp.zeros_like(cnt_ref)

    y = xres_ref[...] + mod_ref[2:3, :] * _rms(proj, g_ref[1:2, :])
    y_ref[...] = y
    h = _rms(y, g_ref[2:3, :]) * (1 + mod_ref[4:5, :]) + mod_ref[3:4, :]
    h_ref[...] = h
    logits = jnp.dot(h, rw_ref[...], preferred_element_type=F32,
                     precision=lax.Precision.HIGHEST) + rb_ref[...]
    tm = logits.shape[0]
    lane_e = lax.broadcasted_iota(jnp.int32, logits.shape, 1)
    lane_o = lax.broadcasted_iota(jnp.int32, (tm, LANES), 1)
    top_i = jnp.zeros((tm, LANES), jnp.int32)
    top_p = jnp.zeros((tm, LANES), F32)
    rest = logits
    first = None
    den = None
    hits = []
    for k in range(TOP_K):
        m = jnp.max(rest, axis=-1, keepdims=True)
        idx = jnp.min(jnp.where(rest == m, lane_e, N_EXPERTS), axis=-1, keepdims=True)
        hit = lane_e == idx
        hits.append(hit)
        rest = jnp.where(hit, -jnp.inf, rest)
        if k == 0:
            first = m
        p = jnp.exp(m - first)
        den = p if k == 0 else den + p
        top_i = jnp.where(lane_o == k, idx, top_i)
        top_p = jnp.where(lane_o == k, p, top_p)
    tg_ref[...] = top_p / den
    onehot = jnp.zeros(logits.shape, F32)
    for hit in hits:
        onehot = onehot + hit.astype(F32)
    tri = (lax.broadcasted_iota(jnp.int32, (tm, tm), 0)
           > lax.broadcasted_iota(jnp.int32, (tm, tm), 1)).astype(BF16)
    carry = cnt_ref[0:1, 0:N_EXPERTS]
    before = jnp.dot(tri, onehot.astype(BF16), preferred_element_type=F32) + carry
    for k, hit in enumerate(hits):
        rank = jnp.sum(jnp.where(hit, before, 0.0), axis=-1, keepdims=True).astype(jnp.int32)
        top_i = jnp.where(lane_o == TOP_K + k, rank, top_i)
    ti_ref[...] = top_i
    cnt_ref[0:1, 0:N_EXPERTS] = carry + jnp.sum(onehot, axis=0, keepdims=True)


def _attn_out_kernel(oc_ref, ol_ref, w_ref, xres_ref, g_ref, mod_ref, rw_ref, rb_ref,
                     y_ref, h_ref, ti_ref, tg_ref, cnt_ref, *, tm, n_ctx):
    is_latent = pl.program_id(0) * tm >= n_ctx
    proj = None
    for kh in range(N_KV_HEADS):
        o = jnp.where(is_latent, ol_ref[kh], oc_ref[kh])
        part = jnp.dot(o, w_ref[kh], preferred_element_type=F32)
        proj = part if proj is None else proj + part
    _post_epilogue(proj, xres_ref, g_ref, mod_ref, rw_ref, rb_ref, y_ref, h_ref, ti_ref, tg_ref, cnt_ref)


def _conv_out_kernel(bg_ref, u_ref, up_ref, un_ref, cw_ref, w_ref, xres_ref, g_ref, mod_ref,
                     rw_ref, rb_ref, y_ref, h_ref, ti_ref, tg_ref, cnt_ref, *, tm, n_ctx, seq, dec_seq):
    i = pl.program_id(0)
    u = u_ref[...]
    row = lax.broadcasted_iota(jnp.int32, (tm, 1), 0)
    r = i * tm + row
    pos = jnp.where(r < n_ctx, r % seq, (r - n_ctx) % dec_seq)
    last = jnp.where(r < n_ctx, seq - 1, dec_seq - 1)
    prev = jnp.where(row == 0, up_ref[7:8, :], pltpu.roll(u, 1, 0))
    prev = jnp.where(pos == 0, 0.0, prev)
    nxt = jnp.where(row == tm - 1, un_ref[0:1, :], pltpu.roll(u, tm - 1, 0))
    nxt = jnp.where(pos == last, 0.0, nxt)
    conv = prev * cw_ref[0:1, :] + u * cw_ref[1:2, :] + nxt * cw_ref[2:3, :]
    z = (bg_ref[...] * conv).astype(BF16)
    proj = jnp.dot(z, w_ref[...], preferred_element_type=F32)
    _post_epilogue(proj, xres_ref, g_ref, mod_ref, rw_ref, rb_ref, y_ref, h_ref, ti_ref, tg_ref, cnt_ref)


def _mixer_out(name, kernel, lead_args, lead_specs, xres, g4, mod_l, router_w, router_b, *, tm, n_ctx,
               dec_seq):
    N, D = xres.shape
    row = lambda i: (i, 0)
    const = lambda i: (0, 0)
    return pl.pallas_call(
        kernel,
        grid=(N // tm,),
        in_specs=lead_specs + [
            pl.BlockSpec((tm, D), row),
            pl.BlockSpec((4, D), const),
            pl.BlockSpec((None, 6, D), lambda i: (_mod_row(i, tm, n_ctx, dec_seq), 0, 0)),
            pl.BlockSpec((D, N_EXPERTS), const),
            pl.BlockSpec((1, N_EXPERTS), const)],
        out_specs=[pl.BlockSpec((tm, D), row), pl.BlockSpec((tm, D), row),
                   pl.BlockSpec((tm, LANES), row), pl.BlockSpec((tm, LANES), row),
                   pl.BlockSpec((8, LANES), const)],
        out_shape=[jax.ShapeDtypeStruct((N, D), F32), jax.ShapeDtypeStruct((N, D), F32),
                   jax.ShapeDtypeStruct((N, LANES), jnp.int32),
                   jax.ShapeDtypeStruct((N, LANES), F32),
                   jax.ShapeDtypeStruct((8, LANES), F32)],
        name=name,
        compiler_params=_cparams(("arbitrary",)),
    )(*lead_args, xres, g4, mod_l, router_w, router_b)


def _attn_out(o_ctx, o_lat, w_bf, xres, g4, mod_l, router_w, router_b, *, tm, n_ctx, dec_seq):
    D = xres.shape[1]
    tc = n_ctx // tm
    specs = [pl.BlockSpec((N_KV_HEADS, tm, SLAB), lambda i: (0, jnp.minimum(i, tc - 1), 0)),
             pl.BlockSpec((N_KV_HEADS, tm, SLAB), lambda i: (0, jnp.maximum(i - tc, 0), 0)),
             pl.BlockSpec((N_KV_HEADS, SLAB, D), lambda i: (0, 0, 0))]
    kern = functools.partial(_attn_out_kernel, tm=tm, n_ctx=n_ctx)
    return _mixer_out("attn_out", kern, [o_ctx, o_lat, w_bf], specs, xres, g4, mod_l, router_w, router_b,
                      tm=tm, n_ctx=n_ctx, dec_seq=dec_seq)


def _conv_out(bg, u, conv_w, w_bf, xres, g4, mod_l, router_w, router_b, *, tm, n_ctx, seq, dec_seq):
    N, D = xres.shape
    sub = 8
    nsub = N // sub
    specs = [pl.BlockSpec((tm, D), lambda i: (i, 0)),
             pl.BlockSpec((tm, D), lambda i: (i, 0)),
             pl.BlockSpec((sub, D), lambda i: (jnp.maximum(i * (tm // sub) - 1, 0), 0)),
             pl.BlockSpec((sub, D), lambda i: (jnp.minimum((i + 1) * (tm // sub), nsub - 1), 0)),
             pl.BlockSpec((3, D), lambda i: (0, 0)),
             pl.BlockSpec((D, D), lambda i: (0, 0))]
    kern = functools.partial(_conv_out_kernel, tm=tm, n_ctx=n_ctx, seq=seq, dec_seq=dec_seq)
    return _mixer_out("conv_out", kern, [bg, u, u, u, conv_w, w_bf], specs, xres, g4, mod_l, router_w, router_b,
                      tm=tm, n_ctx=n_ctx, dec_seq=dec_seq)


def _routing(top_i, counts, tm):
    N = top_i.shape[0]
    nblk = (N * TOP_K) // tm + N_EXPERTS
    top_e = top_i[:, :TOP_K]
    rank = top_i[:, TOP_K:2 * TOP_K]
    blocks = (counts + tm - 1) // tm
    blk_end = jnp.cumsum(blocks)
    pad_start = (blk_end - blocks) * tm
    experts = jnp.arange(N_EXPERTS, dtype=jnp.int32)
    start = jnp.sum(jnp.where(top_e[:, :, None] == experts, pad_start, 0), axis=-1)
    slot = (start + rank).reshape(N * TOP_K)
    blk = jnp.arange(nblk, dtype=jnp.int32)
    blk_exp = jnp.minimum(jnp.sum((blk[:, None] >= blk_end[None, :]).astype(jnp.int32), axis=1),
                          N_EXPERTS - 1)
    return slot, blk_exp, blk_end[-1:]


def _row_copy(src_ref, dst_ref, s, d, sem):
    return pltpu.make_async_copy(src_ref.at[pl.ds(s, 1), :], dst_ref.at[pl.ds(d, 1), :], sem)


def _dispatch_kernel(slot_ref, h_ref, init_ref, xs_ref, sem, *, tm):
    del init_ref
    base = pl.program_id(0) * tm * TOP_K

    def start(r, c):
        for k in range(TOP_K):
            _row_copy(h_ref, xs_ref, r, slot_ref[base + r * TOP_K + k], sem).start()
        return c

    def wait(r, c):
        for k in range(TOP_K):
            _row_copy(h_ref, xs_ref, r, slot_ref[base + r * TOP_K + k], sem).wait()
        return c

    lax.fori_loop(0, tm, start, 0)
    lax.fori_loop(0, tm, wait, 0)


def _dispatch(slot, h, n_rows, *, tm):
    N, D = h.shape
    return pl.pallas_call(
        functools.partial(_dispatch_kernel, tm=tm),
        grid_spec=pltpu.PrefetchScalarGridSpec(
            num_scalar_prefetch=1, grid=(N // tm,),
            in_specs=[pl.BlockSpec((tm, D), lambda i, s: (i, 0)),
                      pl.BlockSpec(memory_space=pl.ANY)],
            out_specs=pl.BlockSpec(memory_space=pl.ANY),
            scratch_shapes=[pltpu.SemaphoreType.DMA]),
        out_shape=jax.ShapeDtypeStruct((n_rows, D), h.dtype),
        input_output_aliases={2: 0},
        name="moe_dispatch",
        compiler_params=_cparams(("arbitrary",)),
    )(slot, h, jnp.zeros((n_rows, D), h.dtype))


def _expert_kernel(be_ref, nu_ref, x_ref, wg_ref, wu_ref, bgt_ref, bup_ref, wd_ref, bd_ref,
                   y_ref, xb_ref):
    i = pl.program_id(0)
    f = pl.program_id(1)
    used = i < nu_ref[0]

    @pl.when(used & (f == 0))
    def _():
        xb_ref[...] = x_ref[...].astype(BF16)

    @pl.when(used)
    def _():
        xb = xb_ref[...]
        gate = jnp.dot(xb, wg_ref[...], preferred_element_type=F32) + bgt_ref[...]
        up = jnp.dot(xb, wu_ref[...], preferred_element_type=F32) + bup_ref[...]
        gate = jnp.minimum(gate, SWIGLU_LIMIT)
        up = jnp.clip(up, -SWIGLU_LIMIT, SWIGLU_LIMIT)
        hid = (up + 1) * (gate * jax.nn.sigmoid(SWIGLU_ALPHA * gate))
        part = jnp.dot(hid.astype(BF16), wd_ref[...], preferred_element_type=F32)

        @pl.when(f == 0)
        def _():
            y_ref[...] = part + bd_ref[...]

        @pl.when(f != 0)
        def _():
            y_ref[...] += part

    @pl.when(jnp.logical_not(used) & (f == 0))
    def _():
        y_ref[...] = jnp.zeros_like(y_ref)


def _experts(xs, blk_exp, n_used, w_gu, b_gu, w_dn, b_dn, *, layer, tm, tf):
    P, D = xs.shape
    L, E, _, F2 = w_gu.shape
    FF = F2 // 2
    nf = FF // tf
    nblk = P // tm

    def blk(i, nu):
        return jnp.minimum(i, nu[0] - 1)

    def fidx(i, f, nu):
        return jnp.where(i < nu[0], f, nf - 1)

    def exp(i, be, nu):
        return be[blk(i, nu)]

    return pl.pallas_call(
        _expert_kernel,
        grid_spec=pltpu.PrefetchScalarGridSpec(
            num_scalar_prefetch=2, grid=(nblk, nf),
            in_specs=[
                pl.BlockSpec((tm, D), lambda i, f, be, nu: (blk(i, nu), 0)),
                pl.BlockSpec((None, None, D, tf),
                             lambda i, f, be, nu: (layer, exp(i, be, nu), 0, fidx(i, f, nu))),
                pl.BlockSpec((None, None, D, tf),
                             lambda i, f, be, nu: (layer, exp(i, be, nu), 0, nf + fidx(i, f, nu))),
                pl.BlockSpec((None, None, 1, tf),
                             lambda i, f, be, nu: (layer, exp(i, be, nu), 0, fidx(i, f, nu))),
                pl.BlockSpec((None, None, 1, tf),
                             lambda i, f, be, nu: (layer, exp(i, be, nu), 0, nf + fidx(i, f, nu))),
                pl.BlockSpec((None, None, tf, D),
                             lambda i, f, be, nu: (layer, exp(i, be, nu), fidx(i, f, nu), 0)),
                pl.BlockSpec((None, None, 1, D), lambda i, f, be, nu: (layer, exp(i, be, nu), 0, 0))],
            out_specs=pl.BlockSpec((tm, D), lambda i, f, be, nu: (i, 0)),
            scratch_shapes=[pltpu.VMEM((tm, D), BF16)]),
        out_shape=jax.ShapeDtypeStruct((P, D), F32),
        name="moe_experts",
        compiler_params=_cparams(("arbitrary", "arbitrary")),
    )(blk_exp, n_used, xs, w_gu, w_gu, b_gu.reshape(L, E, 1, F2), b_gu.reshape(L, E, 1, F2),
      w_dn, b_dn.reshape(L, E, 1, D))


def _combine_kernel(slot_ref, y_ref, tg_ref, xres_ref, g_ref, mod_ref, o_ref, buf_ref, sem, *, tm):
    base = pl.program_id(0) * tm * TOP_K

    def start(r, c):
        for k in range(TOP_K):
            _row_copy(y_ref, buf_ref.at[k], slot_ref[base + r * TOP_K + k], r, sem).start()
        return c

    def wait(r, c):
        for k in range(TOP_K):
            _row_copy(y_ref, buf_ref.at[k], slot_ref[base + r * TOP_K + k], r, sem).wait()
        return c

    lax.fori_loop(0, tm, start, 0)
    lax.fori_loop(0, tm, wait, 0)
    tg = tg_ref[...]
    f = tg[:, 0:1] * buf_ref[0]
    for k in range(1, TOP_K):
        f = f + tg[:, k:k + 1] * buf_ref[k]
    o_ref[...] = xres_ref[...] + mod_ref[5:6, :] * _rms(f, g_ref[3:4, :])


def _combine(slot, y, top_g, xres, g4, mod_l, *, tm, n_ctx, dec_seq):
    N, D = xres.shape
    return pl.pallas_call(
        functools.partial(_combine_kernel, tm=tm),
        grid_spec=pltpu.PrefetchScalarGridSpec(
            num_scalar_prefetch=1, grid=(N // tm,),
            in_specs=[pl.BlockSpec(memory_space=pl.ANY),
                      pl.BlockSpec((tm, LANES), lambda i, s: (i, 0)),
                      pl.BlockSpec((tm, D), lambda i, s: (i, 0)),
                      pl.BlockSpec((4, D), lambda i, s: (0, 0)),
                      pl.BlockSpec((None, 6, D), lambda i, s: (_mod_row(i, tm, n_ctx, dec_seq), 0, 0))],
            out_specs=pl.BlockSpec((tm, D), lambda i, s: (i, 0)),
            scratch_shapes=[pltpu.VMEM((TOP_K, tm, D), F32), pltpu.SemaphoreType.DMA]),
        out_shape=jax.ShapeDtypeStruct((N, D), F32),
        name="moe_combine",
        compiler_params=_cparams(("arbitrary",)),
    )(slot, y, top_g, xres, g4, mod_l)


def _moe(h, top_i, top_g, counts, xres, g4, mod_l, w_gu, b_gu, w_dn, b_dn, *, layer, tm_e, tf, tm_c,
         n_ctx, dec_seq):
    slot, blk_exp, n_used = _routing(top_i, counts[0, :N_EXPERTS].astype(jnp.int32), tm_e)
    xs = _dispatch(slot, h, blk_exp.shape[0] * tm_e, tm=tm_c)
    y = _experts(xs, blk_exp, n_used, w_gu, b_gu, w_dn, b_dn, layer=layer, tm=tm_e, tf=tf)
    return _combine(slot, y, top_g, xres, g4, mod_l, tm=tm_c, n_ctx=n_ctx, dec_seq=dec_seq)


def kernel(x_prompt, x_sample, cache_k, cache_v, c, c_ctx, norm_g, w_ada, b_ada, w_qkv, w_attn_o,
           attn_sinks, w_conv_in, conv_w, w_conv_out, router_w, router_b, w_gate_up, b_gate_up,
           w_down, b_down):
    batch, seq, D = x_prompt.shape
    dec_batch, dec_seq, _ = x_sample.shape
    depth = norm_g.shape[0]
    n_ctx = batch * seq
    n_lat = dec_batch * dec_seq
    N = n_ctx + n_lat
    nq = N_HEADS * HEAD_DIM
    tm = 256
    tm_e = 512
    tf = min(512, D)
    assert dec_batch + 1 <= MOD_ROWS and seq % tm == 0 and dec_seq % tm == 0 and n_ctx % tm_e == 0
    assert dec_seq % ATT_BLOCK == 0 and n_ctx % ATT_BLOCK == 0 and (N * TOP_K) % tm_e == 0

    x = jnp.concatenate([x_prompt.reshape(n_ctx, D), x_sample.reshape(n_lat, D)], axis=0)
    cond = jnp.zeros((MOD_ROWS, D), F32).at[0].set(c_ctx).at[1:1 + dec_batch].set(c)
    mod = _modulation(cond, w_ada, b_ada)
    rope_tabs = _rope_tables(dec_seq)
    w_gu_bf = w_gate_up.astype(BF16)
    w_dn_bf = w_down.astype(BF16)
    kw = dict(n_ctx=n_ctx, dec_seq=dec_seq)

    new_k, new_v = [], []
    for l in range(depth):
        g4 = norm_g[l]
        mod_l = mod[l]
        if l % 2 == 0:
            a = l // 2
            w = jnp.concatenate([w_qkv[a][:, :nq] * (HEAD_DIM ** -0.5), w_qkv[a][:, nq:]], axis=1)
            q, kh, vh, k32, v32 = _qkv_proj(x, g4[0:1], mod_l, w.astype(BF16), rope_tabs, tm=tm, **kw)
            new_k.append(k32[:n_ctx].reshape(batch, seq, N_KV_HEADS, HEAD_DIM))
            new_v.append(v32[:n_ctx].reshape(batch, seq, N_KV_HEADS, HEAD_DIM))
            ck = jnp.transpose(cache_k[:, a], (0, 2, 1, 3)).astype(BF16)
            cv = jnp.transpose(cache_v[:, a], (0, 2, 1, 3)).astype(BF16)
            o_ctx, o_lat = _attention(q, kh, vh, ck, cv, attn_sinks[a], batch=batch, seq=seq,
                                      dec_batch=dec_batch, dec_seq=dec_seq)
            w_o = w_attn_o[a].astype(BF16).reshape(N_KV_HEADS, SLAB, D)
            x, h, top_i, top_g, counts = _attn_out(o_ctx, o_lat, w_o, x, g4, mod_l, router_w[l],
                                                   router_b[l][None], tm=tm, **kw)
        else:
            ci = l // 2
            bg, u = _conv_in(x, g4[0:1], mod_l, w_conv_in[ci].astype(BF16), tm=tm_e,
                             tc=min(512, D), **kw)
            x, h, top_i, top_g, counts = _conv_out(bg, u, conv_w[ci], w_conv_out[ci].astype(BF16), x, g4,
                                                   mod_l, router_w[l], router_b[l][None], tm=tm, seq=seq,
                                                   **kw)
        x = _moe(h, top_i, top_g, counts, x, g4, mod_l, w_gu_bf, b_gate_up, w_dn_bf, b_down,
                 layer=l, tm_e=tm_e, tf=tf, tm_c=tm, **kw)

    y_prompt = x[:n_ctx].reshape(batch, seq, D)
    y_sample = x[n_ctx:].reshape(dec_batch, dec_seq, D)
    return (y_prompt, y_sample, jnp.stack(new_k, axis=1), jnp.stack(new_v, axis=1))
```

```python
import functools

import jax
import jax.numpy as jnp
from jax import lax
from jax.experimental import pallas as pl
from jax.experimental.pallas import tpu as pltpu

N_HEADS = 32
N_KV_HEADS = 8
HEAD_DIM = 64
GROUP = N_HEADS // N_KV_HEADS
SLAB = GROUP * HEAD_DIM
ATT_BLOCK = 128
GRID_W = 64
ROPE_BASE = 10000.0
N_EXPERTS = 32
TOP_K = 4
SWIGLU_ALPHA = 1.702
SWIGLU_LIMIT = 7.0
RMS_EPS = 1e-6
NEG = -1e30
LANES = 128
MOD_ROWS = 8
DMA_UNROLL = 4

F32 = jnp.float32
BF16 = jnp.bfloat16

VMEM_LIMIT = 56 * 1024 * 1024


def _cparams(sem):
    return pltpu.CompilerParams(dimension_semantics=sem, vmem_limit_bytes=VMEM_LIMIT)


def _rms(x, g):
    return x * lax.rsqrt(jnp.mean(x * x, axis=-1, keepdims=True) + RMS_EPS) * g


def _mod_row(i, tm, n_ctx, dec_seq):
    r = i * tm
    return jnp.where(r < n_ctx, 0, 1 + (r - n_ctx) // dec_seq)


def _mod_kernel(cond_ref, w_ref, b_ref, o_ref):
    c = cond_ref[...]
    s = c * jax.nn.sigmoid(c)
    o_ref[...] = jnp.dot(s, w_ref[...], preferred_element_type=F32,
                         precision=lax.Precision.HIGHEST) + b_ref[...]


def _modulation(cond, w_ada, b_ada):
    L, D, D6 = w_ada.shape
    tn = 1024 if D6 % 1024 == 0 else D6
    out = pl.pallas_call(
        _mod_kernel,
        grid=(L, D6 // tn),
        in_specs=[pl.BlockSpec((MOD_ROWS, D), lambda l, j: (0, 0)),
                  pl.BlockSpec((None, D, tn), lambda l, j: (l, 0, j)),
                  pl.BlockSpec((None, 1, tn), lambda l, j: (l, 0, j))],
        out_specs=pl.BlockSpec((None, MOD_ROWS, tn), lambda l, j: (l, 0, j)),
        out_shape=jax.ShapeDtypeStruct((L, MOD_ROWS, D6), F32),
        name="adaln_mod",
        compiler_params=_cparams(("arbitrary", "arbitrary")),
    )(cond, w_ada, b_ada.reshape(L, 1, D6))
    return out.reshape(L, MOD_ROWS, 6, D)


def _qkv_kernel(x_ref, g_ref, mod_ref, w_ref, c_ref, s1_ref, s2_ref,
                q_ref, kh_ref, vh_ref, k32_ref, v32_ref, *, tm, n_ctx):
    i = pl.program_id(0)
    h = _rms(x_ref[...], g_ref[...]) * (1 + mod_ref[1:2, :]) + mod_ref[0:1, :]
    qkv = jnp.dot(h.astype(BF16), w_ref[...], preferred_element_type=F32)
    is_latent = i * tm >= n_ctx
    cos, sin_lo, sin_hi = c_ref[...], s1_ref[...], s2_ref[...]

    def rope(ch):
        r = ch * cos + pltpu.roll(ch, LANES - 16, 1) * sin_lo + pltpu.roll(ch, 16, 1) * sin_hi
        return jnp.where(is_latent, r, ch)

    nq = N_HEADS * HEAD_DIM
    nkv = N_KV_HEADS * HEAD_DIM
    for kh in range(N_KV_HEADS):
        parts = [rope(qkv[:, kh * SLAB + c * LANES: kh * SLAB + (c + 1) * LANES])
                 for c in range(SLAB // LANES)]
        q_ref[kh] = jnp.concatenate(parts, axis=1).astype(BF16)
    for c in range(nkv // LANES):
        kc = rope(qkv[:, nq + c * LANES: nq + (c + 1) * LANES])
        vc = qkv[:, nq + nkv + c * LANES: nq + nkv + (c + 1) * LANES]
        k32_ref[:, c * LANES:(c + 1) * LANES] = kc
        v32_ref[:, c * LANES:(c + 1) * LANES] = vc
        for s in range(LANES // HEAD_DIM):
            kh_ref[2 * c + s] = kc[:, s * HEAD_DIM:(s + 1) * HEAD_DIM].astype(BF16)
            vh_ref[2 * c + s] = vc[:, s * HEAD_DIM:(s + 1) * HEAD_DIM].astype(BF16)


def _qkv_proj(x, g, mod_l, w_bf, rope_tabs, *, tm, n_ctx, dec_seq):
    N, D = x.shape
    nq = N_HEADS * HEAD_DIM
    nkv = N_KV_HEADS * HEAD_DIM
    cos, sin_lo, sin_hi = rope_tabs
    tiles_per_seq = dec_seq // tm

    def tab_idx(i):
        t = jnp.maximum(i * tm - n_ctx, 0) // tm
        return (t % tiles_per_seq, 0)

    row = lambda i: (i, 0)
    return pl.pallas_call(
        functools.partial(_qkv_kernel, tm=tm, n_ctx=n_ctx),
        grid=(N // tm,),
        in_specs=[pl.BlockSpec((tm, D), row),
                  pl.BlockSpec((1, D), lambda i: (0, 0)),
                  pl.BlockSpec((None, 6, D), lambda i: (_mod_row(i, tm, n_ctx, dec_seq), 0, 0)),
                  pl.BlockSpec((D, nq + 2 * nkv), lambda i: (0, 0)),
                  pl.BlockSpec((tm, LANES), tab_idx),
                  pl.BlockSpec((tm, LANES), tab_idx),
                  pl.BlockSpec((tm, LANES), tab_idx)],
        out_specs=[pl.BlockSpec((N_KV_HEADS, tm, SLAB), lambda i: (0, i, 0)),
                   pl.BlockSpec((N_KV_HEADS, tm, HEAD_DIM), lambda i: (0, i, 0)),
                   pl.BlockSpec((N_KV_HEADS, tm, HEAD_DIM), lambda i: (0, i, 0)),
                   pl.BlockSpec((tm, nkv), row),
                   pl.BlockSpec((tm, nkv), row)],
        out_shape=[jax.ShapeDtypeStruct((N_KV_HEADS, N, SLAB), BF16),
                   jax.ShapeDtypeStruct((N_KV_HEADS, N, HEAD_DIM), BF16),
                   jax.ShapeDtypeStruct((N_KV_HEADS, N, HEAD_DIM), BF16),
                   jax.ShapeDtypeStruct((N, nkv), F32),
                   jax.ShapeDtypeStruct((N, nkv), F32)],
        name="qkv_proj",
        compiler_params=_cparams(("arbitrary",)),
    )(x, g, mod_l, w_bf, cos, sin_lo, sin_hi)


def _rope_tables(dec_seq):
    half = HEAD_DIM // 4
    inv = ROPE_BASE ** (-jnp.arange(half, dtype=F32) / half)
    pos = jnp.arange(dec_seq, dtype=jnp.int32)
    row = (pos // GRID_W).astype(F32)[:, None] * inv
    col = (pos % GRID_W).astype(F32)[:, None] * inv
    z = jnp.zeros_like(row)
    cos = jnp.concatenate([jnp.cos(row), jnp.cos(row), jnp.cos(col), jnp.cos(col)], axis=1)
    sin_lo = jnp.concatenate([-jnp.sin(row), z, -jnp.sin(col), z], axis=1)
    sin_hi = jnp.concatenate([z, jnp.sin(row), z, jnp.sin(col)], axis=1)
    rep = LANES // HEAD_DIM
    return tuple(jnp.tile(t, (1, rep)) for t in (cos, sin_lo, sin_hi))


def _attn_heads(q, k, v, mask, sink_ref, kh):
    T = q.shape[0]
    qs = jnp.concatenate([q[:, g * HEAD_DIM:(g + 1) * HEAD_DIM] for g in range(GROUP)], axis=0)
    s = lax.dot_general(qs, k, (((1,), (1,)), ((), ())), preferred_element_type=F32)
    if mask is not None:
        s = jnp.where(mask, s, NEG)
    head = lax.broadcasted_iota(jnp.int32, (GROUP * T, 1), 0) // T
    sink = jnp.zeros((GROUP * T, 1), F32)
    for g in range(GROUP):
        sink = jnp.where(head == g, sink_ref[kh * GROUP + g], sink)
    m = jnp.maximum(jnp.max(s, axis=-1, keepdims=True), sink)
    p = jnp.exp(s - m)
    den = jnp.sum(p, axis=-1, keepdims=True) + jnp.exp(sink - m)
    o = jnp.dot(p.astype(BF16), v, preferred_element_type=F32) / den
    return jnp.concatenate([o[g * T:(g + 1) * T] for g in range(GROUP)], axis=1)


def _ctx_attn_kernel(sink_ref, q_ref, k_ref, v_ref, o_ref):
    def body(kh, carry):
        o_ref[kh] = _attn_heads(q_ref[kh], k_ref[kh], v_ref[kh], None, sink_ref, kh).astype(BF16)
        return carry

    lax.fori_loop(0, N_KV_HEADS, body, 0, unroll=4)


def _lat_attn_kernel(sink_ref, q_ref, kp_ref, kc_ref, kn_ref, vp_ref, vc_ref, vn_ref,
                     ck_ref, cv_ref, o_ref, *, nb, past):
    i = pl.program_id(1)
    T = ATT_BLOCK
    shape = (GROUP * T, 3 * T + past)
    a = lax.broadcasted_iota(jnp.int32, shape, 0) % T
    col = lax.broadcasted_iota(jnp.int32, shape, 1)
    seg = col // T
    c = col % T
    mask = (((seg == 0) & (c >= a) & (i > 0)) | (seg == 1)
            | ((seg == 2) & (c <= a) & (i < nb - 1)) | (seg >= 3))

    def body(kh, carry):
        k = jnp.concatenate([kp_ref[kh], kc_ref[kh], kn_ref[kh], ck_ref[kh]], axis=0)
        v = jnp.concatenate([vp_ref[kh], vc_ref[kh], vn_ref[kh], cv_ref[kh]], axis=0)
        o_ref[kh] = _attn_heads(q_ref[kh], k, v, mask, sink_ref, kh).astype(BF16)
        return carry

    lax.fori_loop(0, N_KV_HEADS, body, 0, unroll=4)


def _attention(q, kh, vh, ck, cv, sinks, *, batch, seq, dec_batch, dec_seq):
    N = q.shape[1]
    n_ctx = batch * seq
    past = ck.shape[2]
    smem = pl.BlockSpec(memory_space=pltpu.SMEM)
    o_ctx = pl.pallas_call(
        _ctx_attn_kernel,
        grid=(batch,),
        in_specs=[smem,
                  pl.BlockSpec((N_KV_HEADS, seq, SLAB), lambda n: (0, n, 0)),
                  pl.BlockSpec((N_KV_HEADS, seq, HEAD_DIM), lambda n: (0, n, 0)),
                  pl.BlockSpec((N_KV_HEADS, seq, HEAD_DIM), lambda n: (0, n, 0))],
        out_specs=pl.BlockSpec((N_KV_HEADS, seq, SLAB), lambda n: (0, n, 0)),
        out_shape=jax.ShapeDtypeStruct((N_KV_HEADS, n_ctx, SLAB), BF16),
        name="ctx_attn",
        compiler_params=_cparams(("arbitrary",)),
    )(sinks, q, kh, vh)

    T = ATT_BLOCK
    nb = dec_seq // T
    base = n_ctx // T
    cur = lambda b, i: (0, base + b * nb + i, 0)
    prv = lambda b, i: (0, base + b * nb + jnp.maximum(i - 1, 0), 0)
    nxt = lambda b, i: (0, base + b * nb + jnp.minimum(i + 1, nb - 1), 0)
    kv_spec = lambda f: pl.BlockSpec((N_KV_HEADS, T, HEAD_DIM), f)
    ctx_spec = pl.BlockSpec((None, N_KV_HEADS, past, HEAD_DIM), lambda b, i: (b, 0, 0, 0))
    o_lat = pl.pallas_call(
        functools.partial(_lat_attn_kernel, nb=nb, past=past),
        grid=(dec_batch, nb),
        in_specs=[smem,
                  pl.BlockSpec((N_KV_HEADS, T, SLAB), cur),
                  kv_spec(prv), kv_spec(cur), kv_spec(nxt),
                  kv_spec(prv), kv_spec(cur), kv_spec(nxt),
                  ctx_spec, ctx_spec],
        out_specs=pl.BlockSpec((N_KV_HEADS, T, SLAB), lambda b, i: (0, b * nb + i, 0)),
        out_shape=jax.ShapeDtypeStruct((N_KV_HEADS, N - n_ctx, SLAB), BF16),
        name="lat_attn",
        compiler_params=_cparams(("arbitrary", "arbitrary")),
    )(sinks, q, kh, kh, kh, vh, vh, vh, ck, cv)
    return o_ctx, o_lat


def _conv_in_kernel(x_ref, g_ref, mod_ref, wb_ref, wc_ref, wx_ref, bg_ref, u_ref, h_ref):
    @pl.when(pl.program_id(1) == 0)
    def _():
        h = _rms(x_ref[...], g_ref[...]) * (1 + mod_ref[1:2, :]) + mod_ref[0:1, :]
        h_ref[...] = h.astype(BF16)

    h = h_ref[...]
    bg_ref[...] = jnp.dot(h, wb_ref[...], preferred_element_type=F32)
    cg = jnp.dot(h, wc_ref[...], preferred_element_type=F32)
    xin = jnp.dot(h, wx_ref[...], preferred_element_type=F32)
    u_ref[...] = cg * xin


def _conv_in(x, g, mod_l, w_bf, *, tm, tc, n_ctx, dec_seq):
    N, D = x.shape
    nc = D // tc
    wspec = lambda k: pl.BlockSpec((D, tc), lambda i, j: (0, k * nc + j))
    return pl.pallas_call(
        _conv_in_kernel,
        grid=(N // tm, nc),
        in_specs=[pl.BlockSpec((tm, D), lambda i, j: (i, 0)),
                  pl.BlockSpec((1, D), lambda i, j: (0, 0)),
                  pl.BlockSpec((None, 6, D), lambda i, j: (_mod_row(i, tm, n_ctx, dec_seq), 0, 0)),
                  wspec(0), wspec(1), wspec(2)],
        out_specs=[pl.BlockSpec((tm, tc), lambda i, j: (i, j)),
                   pl.BlockSpec((tm, tc), lambda i, j: (i, j))],
        out_shape=[jax.ShapeDtypeStruct((N, D), F32), jax.ShapeDtypeStruct((N, D), F32)],
        scratch_shapes=[pltpu.VMEM((tm, D), BF16)],
        name="conv_in",
        compiler_params=_cparams(("arbitrary", "arbitrary")),
    )(x, g, mod_l, w_bf, w_bf, w_bf)


def _post_epilogue(proj, xres_ref, g_ref, mod_ref, rwt_ref, rb_ref, y_ref, h_ref, ti_ref, tg_ref, cnt_ref):
    @pl.when(pl.program_id(0) == 0)
    def _():
        cnt_ref[...] = jnp.zeros_like(cnt_ref)

    y = xres_ref[...] + mod_ref[2:3, :] * _rms(proj, g_ref[1:2, :])
    y_ref[...] = y
    h = _rms(y, g_ref[2:3, :]) * (1 + mod_ref[4:5, :]) + mod_ref[3:4, :]
    h_ref[...] = h
    logits = lax.dot_general(rwt_ref[...], h, (((1,), (1,)), ((), ())), preferred_element_type=F32,
                             precision=lax.Precision.HIGHEST) + rb_ref[...]
    tm = logits.shape[1]
    row_e = lax.broadcasted_iota(jnp.int32, logits.shape, 0)
    rest = logits
    first = None
    den = None
    hits, ids, probs = [], [], []
    for k in range(TOP_K):
        m = jnp.max(rest, axis=0, keepdims=True)
        idx = jnp.min(jnp.where(rest == m, row_e, N_EXPERTS), axis=0, keepdims=True)
        hit = row_e == idx
        rest = jnp.where(hit, -jnp.inf, rest)
        if k == 0:
            first = m
        p = jnp.exp(m - first)
        den = p if k == 0 else den + p
        hits.append(hit)
        ids.append(idx)
        probs.append(p)
    zero = jnp.zeros((1, tm), F32)
    tg_ref[...] = jnp.concatenate([p / den for p in probs] + [zero] * (8 - TOP_K), axis=0)
    onehot = jnp.zeros(logits.shape, F32)
    for hit in hits:
        onehot = onehot + hit.astype(F32)
    earlier = (lax.broadcasted_iota(jnp.int32, (tm, tm), 0)
               < lax.broadcasted_iota(jnp.int32, (tm, tm), 1)).astype(BF16)
    carry = cnt_ref[:, 0:1]
    before = jnp.dot(onehot.astype(BF16), earlier, preferred_element_type=F32) + carry
    ranks = [jnp.sum(jnp.where(hit, before, 0.0), axis=0, keepdims=True).astype(jnp.int32) for hit in hits]
    ti_ref[...] = jnp.concatenate(ids + ranks, axis=0)
    cnt_ref[...] = jnp.broadcast_to(carry + jnp.sum(onehot, axis=1, keepdims=True), cnt_ref.shape)


def _attn_out_kernel(oc_ref, ol_ref, w_ref, xres_ref, g_ref, mod_ref, rw_ref, rb_ref,
                     y_ref, h_ref, ti_ref, tg_ref, cnt_ref, *, tm, n_ctx):
    is_latent = pl.program_id(0) * tm >= n_ctx
    o = jnp.concatenate([jnp.where(is_latent, ol_ref[kh], oc_ref[kh]) for kh in range(N_KV_HEADS)], axis=1)
    proj = jnp.dot(o, w_ref[...], preferred_element_type=F32)
    _post_epilogue(proj, xres_ref, g_ref, mod_ref, rw_ref, rb_ref, y_ref, h_ref, ti_ref, tg_ref, cnt_ref)


def _conv_out_kernel(bg_ref, u_ref, up_ref, un_ref, cw_ref, w_ref, xres_ref, g_ref, mod_ref,
                     rw_ref, rb_ref, y_ref, h_ref, ti_ref, tg_ref, cnt_ref, *, tm, n_ctx, seq, dec_seq):
    i = pl.program_id(0)
    u = u_ref[...]
    row = lax.broadcasted_iota(jnp.int32, (tm, 1), 0)
    r = i * tm + row
    pos = jnp.where(r < n_ctx, r % seq, (r - n_ctx) % dec_seq)
    last = jnp.where(r < n_ctx, seq - 1, dec_seq - 1)
    prev = jnp.where(row == 0, up_ref[7:8, :], pltpu.roll(u, 1, 0))
    prev = jnp.where(pos == 0, 0.0, prev)
    nxt = jnp.where(row == tm - 1, un_ref[0:1, :], pltpu.roll(u, tm - 1, 0))
    nxt = jnp.where(pos == last, 0.0, nxt)
    conv = prev * cw_ref[0:1, :] + u * cw_ref[1:2, :] + nxt * cw_ref[2:3, :]
    z = (bg_ref[...] * conv).astype(BF16)
    proj = jnp.dot(z, w_ref[...], preferred_element_type=F32)
    _post_epilogue(proj, xres_ref, g_ref, mod_ref, rw_ref, rb_ref, y_ref, h_ref, ti_ref, tg_ref, cnt_ref)


def _mixer_out(name, kernel, lead_args, lead_specs, xres, g4, mod_l, router_w, router_b, *, tm, n_ctx,
               dec_seq):
    N, D = xres.shape
    row = lambda i: (i, 0)
    const = lambda i: (0, 0)
    return pl.pallas_call(
        kernel,
        grid=(N // tm,),
        in_specs=lead_specs + [
            pl.BlockSpec((tm, D), row),
            pl.BlockSpec((4, D), const),
            pl.BlockSpec((None, 6, D), lambda i: (_mod_row(i, tm, n_ctx, dec_seq), 0, 0)),
            pl.BlockSpec((N_EXPERTS, D), const),
            pl.BlockSpec((N_EXPERTS, 1), const)],
        out_specs=[pl.BlockSpec((tm, D), row), pl.BlockSpec((tm, D), row),
                   pl.BlockSpec((8, tm), lambda i: (0, i)), pl.BlockSpec((8, tm), lambda i: (0, i)),
                   pl.BlockSpec((N_EXPERTS, LANES), const)],
        out_shape=[jax.ShapeDtypeStruct((N, D), F32), jax.ShapeDtypeStruct((N, D), F32),
                   jax.ShapeDtypeStruct((8, N), jnp.int32),
                   jax.ShapeDtypeStruct((8, N), F32),
                   jax.ShapeDtypeStruct((N_EXPERTS, LANES), F32)],
        name=name,
        compiler_params=_cparams(("arbitrary",)),
    )(*lead_args, xres, g4, mod_l, router_w, router_b)


def _attn_out(o_ctx, o_lat, w_bf, xres, g4, mod_l, router_w, router_b, *, tm, n_ctx, dec_seq):
    D = xres.shape[1]
    tc = n_ctx // tm
    specs = [pl.BlockSpec((N_KV_HEADS, tm, SLAB), lambda i: (0, jnp.minimum(i, tc - 1), 0)),
             pl.BlockSpec((N_KV_HEADS, tm, SLAB), lambda i: (0, jnp.maximum(i - tc, 0), 0)),
             pl.BlockSpec((N_KV_HEADS * SLAB, D), lambda i: (0, 0))]
    kern = functools.partial(_attn_out_kernel, tm=tm, n_ctx=n_ctx)
    return _mixer_out("attn_out", kern, [o_ctx, o_lat, w_bf], specs, xres, g4, mod_l, router_w, router_b,
                      tm=tm, n_ctx=n_ctx, dec_seq=dec_seq)


def _conv_out(bg, u, conv_w, w_bf, xres, g4, mod_l, router_w, router_b, *, tm, n_ctx, seq, dec_seq):
    N, D = xres.shape
    sub = 8
    nsub = N // sub
    specs = [pl.BlockSpec((tm, D), lambda i: (i, 0)),
             pl.BlockSpec((tm, D), lambda i: (i, 0)),
             pl.BlockSpec((sub, D), lambda i: (jnp.maximum(i * (tm // sub) - 1, 0), 0)),
             pl.BlockSpec((sub, D), lambda i: (jnp.minimum((i + 1) * (tm // sub), nsub - 1), 0)),
             pl.BlockSpec((3, D), lambda i: (0, 0)),
             pl.BlockSpec((D, D), lambda i: (0, 0))]
    kern = functools.partial(_conv_out_kernel, tm=tm, n_ctx=n_ctx, seq=seq, dec_seq=dec_seq)
    return _mixer_out("conv_out", kern, [bg, u, u, u, conv_w, w_bf], specs, xres, g4, mod_l, router_w, router_b,
                      tm=tm, n_ctx=n_ctx, dec_seq=dec_seq)


def _routing(top_i, counts, tm):
    N = top_i.shape[1]
    nblk = (N * TOP_K) // tm + N_EXPERTS
    top_e = top_i[:TOP_K].T
    rank = top_i[TOP_K:2 * TOP_K].T
    blocks = (counts + tm - 1) // tm
    blk_end = jnp.cumsum(blocks)
    pad_start = (blk_end - blocks) * tm
    experts = jnp.arange(N_EXPERTS, dtype=jnp.int32)
    start = jnp.sum(jnp.where(top_e[:, :, None] == experts, pad_start, 0), axis=-1)
    slot = (start + rank).reshape(N * TOP_K)
    blk = jnp.arange(nblk, dtype=jnp.int32)
    blk_exp = jnp.minimum(jnp.sum((blk[:, None] >= blk_end[None, :]).astype(jnp.int32), axis=1),
                          N_EXPERTS - 1)
    pad_len = blocks * tm - counts
    pad_end = jnp.cumsum(pad_len)
    p = jnp.arange(N_EXPERTS * (tm - 1), dtype=jnp.int32)
    owner = jnp.minimum(jnp.sum((p[:, None] >= pad_end[None, :]).astype(jnp.int32), axis=1), N_EXPERTS - 1)
    pick = lambda v: jnp.sum(jnp.where(owner[:, None] == experts, v, 0), axis=-1)
    pad_slot = pick(pad_start + counts) + p - pick(pad_end - pad_len)
    pad_slot = jnp.where(p < pad_end[-1], pad_slot, 0)
    return slot, blk_exp, blk_end[-1:], pad_slot, pad_end[-1:]


def _row_copy(src_ref, dst_ref, s, d, sem):
    return pltpu.make_async_copy(src_ref.at[pl.ds(s, 1), :], dst_ref.at[pl.ds(d, 1), :], sem)


def _dispatch_kernel(slot_ref, pad_ref, npad_ref, nused_ref, h_ref, xs_ref, zero_ref, sem, zsem, *,
                     tm, tm_e, nblk):
    base = pl.program_id(0) * tm * TOP_K

    def start(r, c):
        for k in range(TOP_K):
            _row_copy(h_ref, xs_ref, r, slot_ref[base + r * TOP_K + k], sem).start(priority=k % 2)
        return c

    def wait(r, c):
        for k in range(TOP_K):
            _row_copy(h_ref, xs_ref, r, slot_ref[base + r * TOP_K + k], sem).wait()
        return c

    lax.fori_loop(0, tm, start, 0, unroll=DMA_UNROLL)

    @pl.when(pl.program_id(0) == pl.num_programs(0) - 1)
    def _():
        zero_ref[...] = jnp.zeros_like(zero_ref)
        n_pad = npad_ref[0]
        n_used = nused_ref[0]

        def block_copy(b):
            dst = xs_ref.at[pl.ds(pl.multiple_of(b * tm_e, tm_e), tm_e), :]
            return pltpu.make_async_copy(zero_ref, dst, zsem)

        def row_start(p, c):
            _row_copy(zero_ref, xs_ref, 0, pad_ref[p], zsem).start()
            return c

        def row_wait(p, c):
            _row_copy(zero_ref, xs_ref, 0, pad_ref[p], zsem).wait()
            return c

        def blk_start(b, c):
            block_copy(b).start()
            return c

        def blk_wait(b, c):
            block_copy(b).wait()
            return c

        lax.fori_loop(0, n_pad, row_start, 0)
        lax.fori_loop(n_used, nblk, blk_start, 0)
        lax.fori_loop(0, n_pad, row_wait, 0)
        lax.fori_loop(n_used, nblk, blk_wait, 0)

    lax.fori_loop(0, tm, wait, 0, unroll=DMA_UNROLL)


def _dispatch(slot, pad_slot, n_pad, n_used, h, nblk, *, tm, tm_e):
    N, D = h.shape
    return pl.pallas_call(
        functools.partial(_dispatch_kernel, tm=tm, tm_e=tm_e, nblk=nblk),
        grid_spec=pltpu.PrefetchScalarGridSpec(
            num_scalar_prefetch=4, grid=(N // tm,),
            in_specs=[pl.BlockSpec((tm, D), lambda i, *_: (i, 0))],
            out_specs=pl.BlockSpec(memory_space=pl.ANY),
            scratch_shapes=[pltpu.VMEM((tm_e, D), h.dtype), pltpu.SemaphoreType.DMA,
                            pltpu.SemaphoreType.DMA]),
        out_shape=jax.ShapeDtypeStruct((nblk * tm_e, D), h.dtype),
        name="moe_dispatch",
        compiler_params=_cparams(("arbitrary",)),
    )(slot, pad_slot, n_pad, n_used, h)


def _expert_kernel(be_ref, nu_ref, x_ref, wg_ref, wu_ref, bgt_ref, bup_ref, wd_ref, bd_ref,
                   y_ref, xb_ref):
    i = pl.program_id(0)
    f = pl.program_id(1)
    used = i < nu_ref[0]

    @pl.when(used & (f == 0))
    def _():
        xb_ref[...] = x_ref[...].astype(BF16)

    @pl.when(used)
    def _():
        xb = xb_ref[...]
        gate = jnp.dot(xb, wg_ref[...], preferred_element_type=F32) + bgt_ref[...]
        up = jnp.dot(xb, wu_ref[...], preferred_element_type=F32) + bup_ref[...]
        gate = jnp.minimum(gate, SWIGLU_LIMIT)
        up = jnp.clip(up, -SWIGLU_LIMIT, SWIGLU_LIMIT)
        hid = (up + 1) * (gate * jax.nn.sigmoid(SWIGLU_ALPHA * gate))
        part = jnp.dot(hid.astype(BF16), wd_ref[...], preferred_element_type=F32)

        @pl.when(f == 0)
        def _():
            y_ref[...] = part + bd_ref[...]

        @pl.when(f != 0)
        def _():
            y_ref[...] += part

    @pl.when(jnp.logical_not(used) & (f == 0))
    def _():
        y_ref[...] = jnp.zeros_like(y_ref)


def _experts(xs, blk_exp, n_used, w_gu, b_gu, w_dn, b_dn, *, layer, tm, tf):
    P, D = xs.shape
    L, E, _, F2 = w_gu.shape
    FF = F2 // 2
    nf = FF // tf
    nblk = P // tm

    def blk(i, nu):
        return jnp.minimum(i, nu[0] - 1)

    def fidx(i, f, nu):
        return jnp.where(i < nu[0], f, nf - 1)

    def exp(i, be, nu):
        return be[blk(i, nu)]

    return pl.pallas_call(
        _expert_kernel,
        grid_spec=pltpu.PrefetchScalarGridSpec(
            num_scalar_prefetch=2, grid=(nblk, nf),
            in_specs=[
                pl.BlockSpec((tm, D), lambda i, f, be, nu: (blk(i, nu), 0)),
                pl.BlockSpec((None, None, D, tf),
                             lambda i, f, be, nu: (layer, exp(i, be, nu), 0, fidx(i, f, nu))),
                pl.BlockSpec((None, None, D, tf),
                             lambda i, f, be, nu: (layer, exp(i, be, nu), 0, nf + fidx(i, f, nu))),
                pl.BlockSpec((None, None, 1, tf),
                             lambda i, f, be, nu: (layer, exp(i, be, nu), 0, fidx(i, f, nu))),
                pl.BlockSpec((None, None, 1, tf),
                             lambda i, f, be, nu: (layer, exp(i, be, nu), 0, nf + fidx(i, f, nu))),
                pl.BlockSpec((None, None, tf, D),
                             lambda i, f, be, nu: (layer, exp(i, be, nu), fidx(i, f, nu), 0)),
                pl.BlockSpec((None, None, 1, D), lambda i, f, be, nu: (layer, exp(i, be, nu), 0, 0))],
            out_specs=pl.BlockSpec((tm, D), lambda i, f, be, nu: (i, 0)),
            scratch_shapes=[pltpu.VMEM((tm, D), BF16)]),
        out_shape=jax.ShapeDtypeStruct((P, D), F32),
        name="moe_experts",
        compiler_params=_cparams(("arbitrary", "arbitrary")),
    )(blk_exp, n_used, xs, w_gu, w_gu, b_gu.reshape(L, E, 1, F2), b_gu.reshape(L, E, 1, F2),
      w_dn, b_dn.reshape(L, E, 1, D))


def _combine_kernel(slot_ref, y_ref, tg_ref, xres_ref, g_ref, mod_ref, o_ref, buf_ref, sem, *, tm):
    base = pl.program_id(0) * tm * TOP_K

    def start(r, c):
        for k in range(TOP_K):
            _row_copy(y_ref, buf_ref.at[k], slot_ref[base + r * TOP_K + k], r, sem).start(priority=k % 2)
        return c

    def wait(r, c):
        for k in range(TOP_K):
            _row_copy(y_ref, buf_ref.at[k], slot_ref[base + r * TOP_K + k], r, sem).wait()
        return c

    lax.fori_loop(0, tm, start, 0, unroll=DMA_UNROLL)
    lax.fori_loop(0, tm, wait, 0, unroll=DMA_UNROLL)
    tg = tg_ref[...]
    f = tg[:, 0:1] * buf_ref[0]
    for k in range(1, TOP_K):
        f = f + tg[:, k:k + 1] * buf_ref[k]
    o_ref[...] = xres_ref[...] + mod_ref[5:6, :] * _rms(f, g_ref[3:4, :])


def _combine(slot, y, top_g, xres, g4, mod_l, *, tm, n_ctx, dec_seq):
    N, D = xres.shape
    return pl.pallas_call(
        functools.partial(_combine_kernel, tm=tm),
        grid_spec=pltpu.PrefetchScalarGridSpec(
            num_scalar_prefetch=1, grid=(N // tm,),
            in_specs=[pl.BlockSpec(memory_space=pl.ANY),
                      pl.BlockSpec((tm, TOP_K), lambda i, s: (i, 0)),
                      pl.BlockSpec((tm, D), lambda i, s: (i, 0)),
                      pl.BlockSpec((4, D), lambda i, s: (0, 0)),
                      pl.BlockSpec((None, 6, D), lambda i, s: (_mod_row(i, tm, n_ctx, dec_seq), 0, 0))],
            out_specs=pl.BlockSpec((tm, D), lambda i, s: (i, 0)),
            scratch_shapes=[pltpu.VMEM((TOP_K, tm, D), F32), pltpu.SemaphoreType.DMA]),
        out_shape=jax.ShapeDtypeStruct((N, D), F32),
        name="moe_combine",
        compiler_params=_cparams(("arbitrary",)),
    )(slot, y, top_g, xres, g4, mod_l)


def _moe(h, top_i, top_g, counts, xres, g4, mod_l, w_gu, b_gu, w_dn, b_dn, *, layer, tm_e, tf, tm_c,
         n_ctx, dec_seq):
    slot, blk_exp, n_used, pad_slot, n_pad = _routing(top_i, counts[:, 0].astype(jnp.int32), tm_e)
    top_g = top_g[:TOP_K].T
    xs = _dispatch(slot, pad_slot, n_pad, n_used, h, blk_exp.shape[0], tm=tm_c, tm_e=tm_e)
    y = _experts(xs, blk_exp, n_used, w_gu, b_gu, w_dn, b_dn, layer=layer, tm=tm_e, tf=tf)
    return _combine(slot, y, top_g, xres, g4, mod_l, tm=tm_c, n_ctx=n_ctx, dec_seq=dec_seq)


def kernel(x_prompt, x_sample, cache_k, cache_v, c, c_ctx, norm_g, w_ada, b_ada, w_qkv, w_attn_o,
           attn_sinks, w_conv_in, conv_w, w_conv_out, router_w, router_b, w_gate_up, b_gate_up,
           w_down, b_down):
    batch, seq, D = x_prompt.shape
    dec_batch, dec_seq, _ = x_sample.shape
    depth = norm_g.shape[0]
    n_ctx = batch * seq
    n_lat = dec_batch * dec_seq
    N = n_ctx + n_lat
    nq = N_HEADS * HEAD_DIM
    tm = 256
    tm_e = 512
    tf = min(1024, D)
    assert dec_batch + 1 <= MOD_ROWS and seq % tm == 0 and dec_seq % tm == 0 and n_ctx % tm_e == 0
    assert dec_seq % ATT_BLOCK == 0 and n_ctx % ATT_BLOCK == 0 and (N * TOP_K) % tm_e == 0

    x = jnp.concatenate([x_prompt.reshape(n_ctx, D), x_sample.reshape(n_lat, D)], axis=0)
    cond = jnp.zeros((MOD_ROWS, D), F32).at[0].set(c_ctx).at[1:1 + dec_batch].set(c)
    mod = _modulation(cond, w_ada, b_ada)
    rope_tabs = _rope_tables(dec_seq)
    w_gu_bf = w_gate_up.astype(BF16)
    w_dn_bf = w_down.astype(BF16)
    kw = dict(n_ctx=n_ctx, dec_seq=dec_seq)

    new_k, new_v = [], []
    for l in range(depth):
        g4 = norm_g[l]
        mod_l = mod[l]
        if l % 2 == 0:
            a = l // 2
            w = jnp.concatenate([w_qkv[a][:, :nq] * (HEAD_DIM ** -0.5), w_qkv[a][:, nq:]], axis=1)
            q, kh, vh, k32, v32 = _qkv_proj(x, g4[0:1], mod_l, w.astype(BF16), rope_tabs, tm=tm, **kw)
            new_k.append(k32[:n_ctx].reshape(batch, seq, N_KV_HEADS, HEAD_DIM))
            new_v.append(v32[:n_ctx].reshape(batch, seq, N_KV_HEADS, HEAD_DIM))
            ck = jnp.transpose(cache_k[:, a], (0, 2, 1, 3)).astype(BF16)
            cv = jnp.transpose(cache_v[:, a], (0, 2, 1, 3)).astype(BF16)
            o_ctx, o_lat = _attention(q, kh, vh, ck, cv, attn_sinks[a], batch=batch, seq=seq,
                                      dec_batch=dec_batch, dec_seq=dec_seq)
            w_o = w_attn_o[a].astype(BF16)
            x, h, top_i, top_g, counts = _attn_out(o_ctx, o_lat, w_o, x, g4, mod_l, router_w[l].T,
                                                   router_b[l][:, None], tm=tm, **kw)
        else:
            ci = l // 2
            bg, u = _conv_in(x, g4[0:1], mod_l, w_conv_in[ci].astype(BF16), tm=tm_e,
                             tc=min(512, D), **kw)
            x, h, top_i, top_g, counts = _conv_out(bg, u, conv_w[ci], w_conv_out[ci].astype(BF16), x, g4,
                                                   mod_l, router_w[l].T, router_b[l][:, None], tm=tm, seq=seq,
                                                   **kw)
        x = _moe(h, top_i, top_g, counts, x, g4, mod_l, w_gu_bf, b_gate_up, w_dn_bf, b_down,
                 layer=l, tm_e=tm_e, tf=tf, tm_c=tm, **kw)

    y_prompt = x[:n_ctx].reshape(batch, seq, D)
    y_sample = x[n_ctx:].reshape(dec_batch, dec_seq, D)
    return (y_prompt, y_sample, jnp.stack(new_k, axis=1), jnp.stack(new_v, axis=1))
```

```python
import functools

import jax
import jax.numpy as jnp
from jax import lax
from jax.experimental import pallas as pl
from jax.experimental.pallas import tpu as pltpu

N_HEADS = 32
N_KV_HEADS = 8
HEAD_DIM = 64
GROUP = N_HEADS // N_KV_HEADS
SLAB = GROUP * HEAD_DIM
ATT_BLOCK = 128
GRID_W = 64
ROPE_BASE = 10000.0
N_EXPERTS = 32
TOP_K = 4
SWIGLU_ALPHA = 1.702
SWIGLU_LIMIT = 7.0
RMS_EPS = 1e-6
NEG = -1e30
LANES = 128
MOD_ROWS = 8
DMA_UNROLL = 4
EXPERT_SUB = 128
ROUTE_GROUPS = 2
EXPERT_GROUP = 256

F32 = jnp.float32
BF16 = jnp.bfloat16

VMEM_LIMIT = 56 * 1024 * 1024


def _cparams(sem):
    return pltpu.CompilerParams(dimension_semantics=sem, vmem_limit_bytes=VMEM_LIMIT)


def _rms(x, g):
    return x * lax.rsqrt(jnp.mean(x * x, axis=-1, keepdims=True) + RMS_EPS) * g


def _mod_row(i, tm, n_ctx, dec_seq):
    r = i * tm
    return jnp.where(r < n_ctx, 0, 1 + (r - n_ctx) // dec_seq)


def _mod_kernel(cond_ref, w_ref, b_ref, o_ref):
    c = cond_ref[...]
    s = c * jax.nn.sigmoid(c)
    o_ref[...] = jnp.dot(s, w_ref[...], preferred_element_type=F32,
                         precision=lax.Precision.HIGHEST) + b_ref[...]


def _modulation(cond, w_ada, b_ada):
    L, D, D6 = w_ada.shape
    tn = 1024 if D6 % 1024 == 0 else D6
    out = pl.pallas_call(
        _mod_kernel,
        grid=(L, D6 // tn),
        in_specs=[pl.BlockSpec((MOD_ROWS, D), lambda l, j: (0, 0)),
                  pl.BlockSpec((None, D, tn), lambda l, j: (l, 0, j)),
                  pl.BlockSpec((None, 1, tn), lambda l, j: (l, 0, j))],
        out_specs=pl.BlockSpec((None, MOD_ROWS, tn), lambda l, j: (l, 0, j)),
        out_shape=jax.ShapeDtypeStruct((L, MOD_ROWS, D6), F32),
        name="adaln_mod",
        compiler_params=_cparams(("arbitrary", "arbitrary")),
    )(cond, w_ada, b_ada.reshape(L, 1, D6))
    return out.reshape(L, MOD_ROWS, 6, D)


def _qkv_kernel(x_ref, g_ref, mod_ref, w_ref, c_ref, s1_ref, s2_ref,
                q_ref, kh_ref, vh_ref, k32_ref, v32_ref, *, tm, n_ctx):
    i = pl.program_id(0)
    h = _rms(x_ref[...], g_ref[...]) * (1 + mod_ref[1:2, :]) + mod_ref[0:1, :]
    qkv = jnp.dot(h.astype(BF16), w_ref[...], preferred_element_type=F32)
    is_latent = i * tm >= n_ctx
    cos, sin_lo, sin_hi = c_ref[...], s1_ref[...], s2_ref[...]

    def rope(ch):
        r = ch * cos + pltpu.roll(ch, LANES - 16, 1) * sin_lo + pltpu.roll(ch, 16, 1) * sin_hi
        return jnp.where(is_latent, r, ch)

    nq = N_HEADS * HEAD_DIM
    nkv = N_KV_HEADS * HEAD_DIM
    for kh in range(N_KV_HEADS):
        parts = [rope(qkv[:, kh * SLAB + c * LANES: kh * SLAB + (c + 1) * LANES])
                 for c in range(SLAB // LANES)]
        q_ref[kh] = jnp.concatenate(parts, axis=1).astype(BF16)
    for c in range(nkv // LANES):
        kc = rope(qkv[:, nq + c * LANES: nq + (c + 1) * LANES])
        vc = qkv[:, nq + nkv + c * LANES: nq + nkv + (c + 1) * LANES]
        k32_ref[:, c * LANES:(c + 1) * LANES] = kc
        v32_ref[:, c * LANES:(c + 1) * LANES] = vc
        for s in range(LANES // HEAD_DIM):
            kh_ref[2 * c + s] = kc[:, s * HEAD_DIM:(s + 1) * HEAD_DIM].astype(BF16)
            vh_ref[2 * c + s] = vc[:, s * HEAD_DIM:(s + 1) * HEAD_DIM].astype(BF16)


def _qkv_proj(x, g, mod_l, w_bf, rope_tabs, *, tm, n_ctx, dec_seq):
    N, D = x.shape
    nq = N_HEADS * HEAD_DIM
    nkv = N_KV_HEADS * HEAD_DIM
    cos, sin_lo, sin_hi = rope_tabs
    tiles_per_seq = dec_seq // tm

    def tab_idx(i):
        t = jnp.maximum(i * tm - n_ctx, 0) // tm
        return (t % tiles_per_seq, 0)

    row = lambda i: (i, 0)
    return pl.pallas_call(
        functools.partial(_qkv_kernel, tm=tm, n_ctx=n_ctx),
        grid=(N // tm,),
        in_specs=[pl.BlockSpec((tm, D), row),
                  pl.BlockSpec((1, D), lambda i: (0, 0)),
                  pl.BlockSpec((None, 6, D), lambda i: (_mod_row(i, tm, n_ctx, dec_seq), 0, 0)),
                  pl.BlockSpec((D, nq + 2 * nkv), lambda i: (0, 0)),
                  pl.BlockSpec((tm, LANES), tab_idx),
                  pl.BlockSpec((tm, LANES), tab_idx),
                  pl.BlockSpec((tm, LANES), tab_idx)],
        out_specs=[pl.BlockSpec((N_KV_HEADS, tm, SLAB), lambda i: (0, i, 0)),
                   pl.BlockSpec((N_KV_HEADS, tm, HEAD_DIM), lambda i: (0, i, 0)),
                   pl.BlockSpec((N_KV_HEADS, tm, HEAD_DIM), lambda i: (0, i, 0)),
                   pl.BlockSpec((tm, nkv), row),
                   pl.BlockSpec((tm, nkv), row)],
        out_shape=[jax.ShapeDtypeStruct((N_KV_HEADS, N, SLAB), BF16),
                   jax.ShapeDtypeStruct((N_KV_HEADS, N, HEAD_DIM), BF16),
                   jax.ShapeDtypeStruct((N_KV_HEADS, N, HEAD_DIM), BF16),
                   jax.ShapeDtypeStruct((N, nkv), F32),
                   jax.ShapeDtypeStruct((N, nkv), F32)],
        name="qkv_proj",
        compiler_params=_cparams(("arbitrary",)),
    )(x, g, mod_l, w_bf, cos, sin_lo, sin_hi)


def _rope_tables(dec_seq):
    half = HEAD_DIM // 4
    inv = ROPE_BASE ** (-jnp.arange(half, dtype=F32) / half)
    pos = jnp.arange(dec_seq, dtype=jnp.int32)
    row = (pos // GRID_W).astype(F32)[:, None] * inv
    col = (pos % GRID_W).astype(F32)[:, None] * inv
    z = jnp.zeros_like(row)
    cos = jnp.concatenate([jnp.cos(row), jnp.cos(row), jnp.cos(col), jnp.cos(col)], axis=1)
    sin_lo = jnp.concatenate([-jnp.sin(row), z, -jnp.sin(col), z], axis=1)
    sin_hi = jnp.concatenate([z, jnp.sin(row), z, jnp.sin(col)], axis=1)
    rep = LANES // HEAD_DIM
    return tuple(jnp.tile(t, (1, rep)) for t in (cos, sin_lo, sin_hi))


def _attn_heads(q, k, v, mask, sink_ref, kh):
    T = q.shape[0]
    qs = jnp.concatenate([q[:, g * HEAD_DIM:(g + 1) * HEAD_DIM] for g in range(GROUP)], axis=0)
    s = lax.dot_general(qs, k, (((1,), (1,)), ((), ())), preferred_element_type=F32)
    if mask is not None:
        s = jnp.where(mask, s, NEG)
    head = lax.broadcasted_iota(jnp.int32, (GROUP * T, 1), 0) // T
    sink = jnp.zeros((GROUP * T, 1), F32)
    for g in range(GROUP):
        sink = jnp.where(head == g, sink_ref[kh * GROUP + g], sink)
    m = jnp.maximum(jnp.max(s, axis=-1, keepdims=True), sink)
    p = jnp.exp(s - m)
    den = jnp.sum(p, axis=-1, keepdims=True) + jnp.exp(sink - m)
    o = jnp.dot(p.astype(BF16), v, preferred_element_type=F32) / den
    return jnp.concatenate([o[g * T:(g + 1) * T] for g in range(GROUP)], axis=1)


def _ctx_attn_kernel(sink_ref, q_ref, k_ref, v_ref, o_ref):
    def body(kh, carry):
        o_ref[kh] = _attn_heads(q_ref[kh], k_ref[kh], v_ref[kh], None, sink_ref, kh).astype(BF16)
        return carry

    lax.fori_loop(0, N_KV_HEADS, body, 0, unroll=4)


def _lat_attn_kernel(sink_ref, q_ref, kp_ref, kc_ref, kn_ref, vp_ref, vc_ref, vn_ref,
                     ck_ref, cv_ref, o_ref, *, nb, past):
    i = pl.program_id(1)
    T = ATT_BLOCK
    shape = (GROUP * T, 3 * T + past)
    a = lax.broadcasted_iota(jnp.int32, shape, 0) % T
    col = lax.broadcasted_iota(jnp.int32, shape, 1)
    seg = col // T
    c = col % T
    mask = (((seg == 0) & (c >= a) & (i > 0)) | (seg == 1)
            | ((seg == 2) & (c <= a) & (i < nb - 1)) | (seg >= 3))

    def body(kh, carry):
        k = jnp.concatenate([kp_ref[kh], kc_ref[kh], kn_ref[kh], ck_ref[kh]], axis=0)
        v = jnp.concatenate([vp_ref[kh], vc_ref[kh], vn_ref[kh], cv_ref[kh]], axis=0)
        o_ref[kh] = _attn_heads(q_ref[kh], k, v, mask, sink_ref, kh).astype(BF16)
        return carry

    lax.fori_loop(0, N_KV_HEADS, body, 0, unroll=4)


def _attention(q, kh, vh, ck, cv, sinks, *, batch, seq, dec_batch, dec_seq):
    N = q.shape[1]
    n_ctx = batch * seq
    past = ck.shape[2]
    smem = pl.BlockSpec(memory_space=pltpu.SMEM)
    o_ctx = pl.pallas_call(
        _ctx_attn_kernel,
        grid=(batch,),
        in_specs=[smem,
                  pl.BlockSpec((N_KV_HEADS, seq, SLAB), lambda n: (0, n, 0)),
                  pl.BlockSpec((N_KV_HEADS, seq, HEAD_DIM), lambda n: (0, n, 0)),
                  pl.BlockSpec((N_KV_HEADS, seq, HEAD_DIM), lambda n: (0, n, 0))],
        out_specs=pl.BlockSpec((N_KV_HEADS, seq, SLAB), lambda n: (0, n, 0)),
        out_shape=jax.ShapeDtypeStruct((N_KV_HEADS, n_ctx, SLAB), BF16),
        name="ctx_attn",
        compiler_params=_cparams(("arbitrary",)),
    )(sinks, q, kh, vh)

    T = ATT_BLOCK
    nb = dec_seq // T
    base = n_ctx // T
    cur = lambda b, i: (0, base + b * nb + i, 0)
    prv = lambda b, i: (0, base + b * nb + jnp.maximum(i - 1, 0), 0)
    nxt = lambda b, i: (0, base + b * nb + jnp.minimum(i + 1, nb - 1), 0)
    kv_spec = lambda f: pl.BlockSpec((N_KV_HEADS, T, HEAD_DIM), f)
    ctx_spec = pl.BlockSpec((None, N_KV_HEADS, past, HEAD_DIM), lambda b, i: (b, 0, 0, 0))
    o_lat = pl.pallas_call(
        functools.partial(_lat_attn_kernel, nb=nb, past=past),
        grid=(dec_batch, nb),
        in_specs=[smem,
                  pl.BlockSpec((N_KV_HEADS, T, SLAB), cur),
                  kv_spec(prv), kv_spec(cur), kv_spec(nxt),
                  kv_spec(prv), kv_spec(cur), kv_spec(nxt),
                  ctx_spec, ctx_spec],
        out_specs=pl.BlockSpec((N_KV_HEADS, T, SLAB), lambda b, i: (0, b * nb + i, 0)),
        out_shape=jax.ShapeDtypeStruct((N_KV_HEADS, N - n_ctx, SLAB), BF16),
        name="lat_attn",
        compiler_params=_cparams(("arbitrary", "arbitrary")),
    )(sinks, q, kh, kh, kh, vh, vh, vh, ck, cv)
    return o_ctx, o_lat


def _conv_in_kernel(x_ref, g_ref, mod_ref, wb_ref, wc_ref, wx_ref, bg_ref, u_ref, h_ref):
    @pl.when(pl.program_id(1) == 0)
    def _():
        h = _rms(x_ref[...], g_ref[...]) * (1 + mod_ref[1:2, :]) + mod_ref[0:1, :]
        h_ref[...] = h.astype(BF16)

    h = h_ref[...]
    bg_ref[...] = jnp.dot(h, wb_ref[...], preferred_element_type=F32)
    cg = jnp.dot(h, wc_ref[...], preferred_element_type=F32)
    xin = jnp.dot(h, wx_ref[...], preferred_element_type=F32)
    u_ref[...] = cg * xin


def _conv_in(x, g, mod_l, w_bf, *, tm, tc, n_ctx, dec_seq):
    N, D = x.shape
    nc = D // tc
    wspec = lambda k: pl.BlockSpec((D, tc), lambda i, j: (0, k * nc + j))
    return pl.pallas_call(
        _conv_in_kernel,
        grid=(N // tm, nc),
        in_specs=[pl.BlockSpec((tm, D), lambda i, j: (i, 0)),
                  pl.BlockSpec((1, D), lambda i, j: (0, 0)),
                  pl.BlockSpec((None, 6, D), lambda i, j: (_mod_row(i, tm, n_ctx, dec_seq), 0, 0)),
                  wspec(0), wspec(1), wspec(2)],
        out_specs=[pl.BlockSpec((tm, tc), lambda i, j: (i, j)),
                   pl.BlockSpec((tm, tc), lambda i, j: (i, j))],
        out_shape=[jax.ShapeDtypeStruct((N, D), F32), jax.ShapeDtypeStruct((N, D), F32)],
        scratch_shapes=[pltpu.VMEM((tm, D), BF16)],
        name="conv_in",
        compiler_params=_cparams(("arbitrary", "arbitrary")),
    )(x, g, mod_l, w_bf, w_bf, w_bf)


def _route_rows(proj, xres, g_ref, mod_ref, rwt_ref, rb_ref, carry):
    y = xres + mod_ref[2:3, :] * _rms(proj, g_ref[1:2, :])
    h = _rms(y, g_ref[2:3, :]) * (1 + mod_ref[4:5, :]) + mod_ref[3:4, :]
    logits = lax.dot_general(rwt_ref[...], h, (((1,), (1,)), ((), ())), preferred_element_type=F32,
                             precision=lax.Precision.HIGHEST) + rb_ref[...]
    tm = logits.shape[1]
    row_e = lax.broadcasted_iota(jnp.int32, logits.shape, 0)
    rest = logits
    first = None
    den = None
    hits, ids, probs = [], [], []
    for k in range(TOP_K):
        m = jnp.max(rest, axis=0, keepdims=True)
        idx = jnp.min(jnp.where(rest == m, row_e, N_EXPERTS), axis=0, keepdims=True)
        hit = row_e == idx
        rest = jnp.where(hit, -jnp.inf, rest)
        if k == 0:
            first = m
        p = jnp.exp(m - first)
        den = p if k == 0 else den + p
        hits.append(hit)
        ids.append(idx)
        probs.append(p)
    zero = jnp.zeros((1, tm), F32)
    tg = jnp.concatenate([p / den for p in probs] + [zero] * (8 - TOP_K), axis=0)
    onehot = jnp.zeros(logits.shape, F32)
    for hit in hits:
        onehot = onehot + hit.astype(F32)
    earlier = (lax.broadcasted_iota(jnp.int32, (tm, tm), 0)
               < lax.broadcasted_iota(jnp.int32, (tm, tm), 1)).astype(BF16)
    before = jnp.dot(onehot.astype(BF16), earlier, preferred_element_type=F32) + carry
    ranks = [jnp.sum(jnp.where(hit, before, 0.0), axis=0, keepdims=True).astype(jnp.int32) for hit in hits]
    ti = jnp.concatenate(ids + ranks, axis=0)
    return y, h, ti, tg, carry + jnp.sum(onehot, axis=1, keepdims=True)


def _post_epilogue(proj, xres_ref, g_ref, mod_ref, rwt_ref, rb_ref, y_ref, h_ref, ti_ref, tg_ref, cnt_ref):
    @pl.when(pl.program_id(0) == 0)
    def _():
        cnt_ref[...] = jnp.zeros_like(cnt_ref)

    tm = proj.shape[0]
    grp = tm // ROUTE_GROUPS
    carry = cnt_ref[:, 0:1]
    for s in range(ROUTE_GROUPS):
        rows = slice(s * grp, (s + 1) * grp)
        y, h, ti, tg, carry = _route_rows(proj[rows], xres_ref[rows, :], g_ref, mod_ref, rwt_ref, rb_ref, carry)
        y_ref[rows, :] = y
        h_ref[rows, :] = h
        ti_ref[:, rows] = ti
        tg_ref[:, rows] = tg
    cnt_ref[...] = jnp.broadcast_to(carry, cnt_ref.shape)


def _attn_out_kernel(oc_ref, ol_ref, w_ref, xres_ref, g_ref, mod_ref, rw_ref, rb_ref,
                     y_ref, h_ref, ti_ref, tg_ref, cnt_ref, *, tm, n_ctx):
    is_latent = pl.program_id(0) * tm >= n_ctx
    o = jnp.concatenate([jnp.where(is_latent, ol_ref[kh], oc_ref[kh]) for kh in range(N_KV_HEADS)], axis=1)
    proj = jnp.dot(o, w_ref[...], preferred_element_type=F32)
    _post_epilogue(proj, xres_ref, g_ref, mod_ref, rw_ref, rb_ref, y_ref, h_ref, ti_ref, tg_ref, cnt_ref)


def _conv_out_kernel(bg_ref, u_ref, up_ref, un_ref, cw_ref, w_ref, xres_ref, g_ref, mod_ref,
                     rw_ref, rb_ref, y_ref, h_ref, ti_ref, tg_ref, cnt_ref, *, tm, n_ctx, seq, dec_seq):
    i = pl.program_id(0)
    u = u_ref[...]
    row = lax.broadcasted_iota(jnp.int32, (tm, 1), 0)
    r = i * tm + row
    pos = jnp.where(r < n_ctx, r % seq, (r - n_ctx) % dec_seq)
    last = jnp.where(r < n_ctx, seq - 1, dec_seq - 1)
    prev = jnp.where(row == 0, up_ref[7:8, :], pltpu.roll(u, 1, 0))
    prev = jnp.where(pos == 0, 0.0, prev)
    nxt = jnp.where(row == tm - 1, un_ref[0:1, :], pltpu.roll(u, tm - 1, 0))
    nxt = jnp.where(pos == last, 0.0, nxt)
    conv = prev * cw_ref[0:1, :] + u * cw_ref[1:2, :] + nxt * cw_ref[2:3, :]
    z = (bg_ref[...] * conv).astype(BF16)
    proj = jnp.dot(z, w_ref[...], preferred_element_type=F32)
    _post_epilogue(proj, xres_ref, g_ref, mod_ref, rw_ref, rb_ref, y_ref, h_ref, ti_ref, tg_ref, cnt_ref)


def _mixer_out(name, kernel, lead_args, lead_specs, xres, g4, mod_l, router_w, router_b, *, tm, n_ctx,
               dec_seq):
    N, D = xres.shape
    row = lambda i: (i, 0)
    const = lambda i: (0, 0)
    return pl.pallas_call(
        kernel,
        grid=(N // tm,),
        in_specs=lead_specs + [
            pl.BlockSpec((tm, D), row),
            pl.BlockSpec((4, D), const),
            pl.BlockSpec((None, 6, D), lambda i: (_mod_row(i, tm, n_ctx, dec_seq), 0, 0)),
            pl.BlockSpec((N_EXPERTS, D), const),
            pl.BlockSpec((N_EXPERTS, 1), const)],
        out_specs=[pl.BlockSpec((tm, D), row), pl.BlockSpec((tm, D), row),
                   pl.BlockSpec((8, tm), lambda i: (0, i)), pl.BlockSpec((8, tm), lambda i: (0, i)),
                   pl.BlockSpec((N_EXPERTS, LANES), const)],
        out_shape=[jax.ShapeDtypeStruct((N, D), F32), jax.ShapeDtypeStruct((N, D), F32),
                   jax.ShapeDtypeStruct((8, N), jnp.int32),
                   jax.ShapeDtypeStruct((8, N), F32),
                   jax.ShapeDtypeStruct((N_EXPERTS, LANES), F32)],
        name=name,
        compiler_params=_cparams(("arbitrary",)),
    )(*lead_args, xres, g4, mod_l, router_w, router_b)


def _attn_out(o_ctx, o_lat, w_bf, xres, g4, mod_l, router_w, router_b, *, tm, n_ctx, dec_seq):
    D = xres.shape[1]
    tc = n_ctx // tm
    specs = [pl.BlockSpec((N_KV_HEADS, tm, SLAB), lambda i: (0, jnp.minimum(i, tc - 1), 0)),
             pl.BlockSpec((N_KV_HEADS, tm, SLAB), lambda i: (0, jnp.maximum(i - tc, 0), 0)),
             pl.BlockSpec((N_KV_HEADS * SLAB, D), lambda i: (0, 0))]
    kern = functools.partial(_attn_out_kernel, tm=tm, n_ctx=n_ctx)
    return _mixer_out("attn_out", kern, [o_ctx, o_lat, w_bf], specs, xres, g4, mod_l, router_w, router_b,
                      tm=tm, n_ctx=n_ctx, dec_seq=dec_seq)


def _conv_out(bg, u, conv_w, w_bf, xres, g4, mod_l, router_w, router_b, *, tm, n_ctx, seq, dec_seq):
    N, D = xres.shape
    sub = 8
    nsub = N // sub
    specs = [pl.BlockSpec((tm, D), lambda i: (i, 0)),
             pl.BlockSpec((tm, D), lambda i: (i, 0)),
             pl.BlockSpec((sub, D), lambda i: (jnp.maximum(i * (tm // sub) - 1, 0), 0)),
             pl.BlockSpec((sub, D), lambda i: (jnp.minimum((i + 1) * (tm // sub), nsub - 1), 0)),
             pl.BlockSpec((3, D), lambda i: (0, 0)),
             pl.BlockSpec((D, D), lambda i: (0, 0))]
    kern = functools.partial(_conv_out_kernel, tm=tm, n_ctx=n_ctx, seq=seq, dec_seq=dec_seq)
    return _mixer_out("conv_out", kern, [bg, u, u, u, conv_w, w_bf], specs, xres, g4, mod_l, router_w, router_b,
                      tm=tm, n_ctx=n_ctx, dec_seq=dec_seq)


def _routing(top_i, counts, tm):
    N = top_i.shape[1]
    nblk = (N * TOP_K) // tm + N_EXPERTS
    top_e = top_i[:TOP_K].T
    rank = top_i[TOP_K:2 * TOP_K].T
    blocks = (counts + tm - 1) // tm
    blk_end = jnp.cumsum(blocks)
    pad_start = (blk_end - blocks) * tm
    experts = jnp.arange(N_EXPERTS, dtype=jnp.int32)
    start = jnp.sum(jnp.where(top_e[:, :, None] == experts, pad_start, 0), axis=-1)
    slot = (start + rank).reshape(N * TOP_K)
    blk = jnp.arange(nblk, dtype=jnp.int32)
    blk_exp = jnp.minimum(jnp.sum((blk[:, None] >= blk_end[None, :]).astype(jnp.int32), axis=1),
                          N_EXPERTS - 1)
    first_blk = blk_end - blocks
    of_blk = lambda v: jnp.sum(jnp.where(blk_exp[:, None] == experts, v, 0), axis=-1)
    blk_valid = jnp.clip(of_blk(counts) - (blk - of_blk(first_blk)) * tm, 0, tm)
    pad_len = blocks * tm - counts
    pad_end = jnp.cumsum(pad_len)
    p = jnp.arange(N_EXPERTS * (tm - 1), dtype=jnp.int32)
    owner = jnp.minimum(jnp.sum((p[:, None] >= pad_end[None, :]).astype(jnp.int32), axis=1), N_EXPERTS - 1)
    pick = lambda v: jnp.sum(jnp.where(owner[:, None] == experts, v, 0), axis=-1)
    pad_slot = pick(pad_start + counts) + p - pick(pad_end - pad_len)
    pad_slot = jnp.where(p < pad_end[-1], pad_slot, 0)
    return slot, blk_exp, blk_end[-1:], blk_valid, pad_slot, pad_end[-1:]


def _row_copy(src_ref, dst_ref, s, d, sem):
    return pltpu.make_async_copy(src_ref.at[pl.ds(s, 1), :], dst_ref.at[pl.ds(d, 1), :], sem)


def _dispatch_kernel(slot_ref, pad_ref, npad_ref, nused_ref, h_ref, xs_ref, zero_ref, sem, zsem, *,
                     tm, tm_e, nblk):
    base = pl.program_id(0) * tm * TOP_K

    def start(r, c):
        for k in range(TOP_K):
            _row_copy(h_ref, xs_ref, r, slot_ref[base + r * TOP_K + k], sem).start(priority=k % 2)
        return c

    def wait(r, c):
        for k in range(TOP_K):
            _row_copy(h_ref, xs_ref, r, slot_ref[base + r * TOP_K + k], sem).wait()
        return c

    lax.fori_loop(0, tm, start, 0, unroll=DMA_UNROLL)

    @pl.when(pl.program_id(0) == pl.num_programs(0) - 1)
    def _():
        zero_ref[...] = jnp.zeros_like(zero_ref)
        n_pad = npad_ref[0]
        n_used = nused_ref[0]

        def block_copy(b):
            dst = xs_ref.at[pl.ds(pl.multiple_of(b * tm_e, tm_e), tm_e), :]
            return pltpu.make_async_copy(zero_ref, dst, zsem)

        def row_start(p, c):
            _row_copy(zero_ref, xs_ref, 0, pad_ref[p], zsem).start()
            return c

        def row_wait(p, c):
            _row_copy(zero_ref, xs_ref, 0, pad_ref[p], zsem).wait()
            return c

        def blk_start(b, c):
            block_copy(b).start()
            return c

        def blk_wait(b, c):
            block_copy(b).wait()
            return c

        lax.fori_loop(0, n_pad, row_start, 0)
        lax.fori_loop(n_used, nblk, blk_start, 0)
        lax.fori_loop(0, n_pad, row_wait, 0)
        lax.fori_loop(n_used, nblk, blk_wait, 0)

    lax.fori_loop(0, tm, wait, 0, unroll=DMA_UNROLL)


def _dispatch(slot, pad_slot, n_pad, n_used, h, nblk, *, tm, tm_e):
    N, D = h.shape
    return pl.pallas_call(
        functools.partial(_dispatch_kernel, tm=tm, tm_e=tm_e, nblk=nblk),
        grid_spec=pltpu.PrefetchScalarGridSpec(
            num_scalar_prefetch=4, grid=(N // tm,),
            in_specs=[pl.BlockSpec((tm, D), lambda i, *_: (i, 0))],
            out_specs=pl.BlockSpec(memory_space=pl.ANY),
            scratch_shapes=[pltpu.VMEM((tm_e, D), h.dtype), pltpu.SemaphoreType.DMA,
                            pltpu.SemaphoreType.DMA]),
        out_shape=jax.ShapeDtypeStruct((nblk * tm_e, D), h.dtype),
        name="moe_dispatch",
        compiler_params=_cparams(("arbitrary",)),
    )(slot, pad_slot, n_pad, n_used, h)


def _expert_kernel(be_ref, nu_ref, nv_ref, x_ref, wg_ref, wu_ref, bgt_ref, bup_ref, wd_ref, bd_ref,
                   y_ref, xb_ref, *, tm, sub, grp):
    i = pl.program_id(0)
    f = pl.program_id(1)
    used = i < nu_ref[0]
    n_valid = nv_ref[i]

    def mlp(rows):
        xb = xb_ref[rows, :]
        gate = jnp.dot(xb, wg_ref[...], preferred_element_type=F32) + bgt_ref[...]
        up = jnp.dot(xb, wu_ref[...], preferred_element_type=F32) + bup_ref[...]
        gate = jnp.minimum(gate, SWIGLU_LIMIT)
        up = jnp.clip(up, -SWIGLU_LIMIT, SWIGLU_LIMIT)
        hid = (up + 1) * (gate * jax.nn.sigmoid(SWIGLU_ALPHA * gate))
        y_ref[rows, :] += jnp.dot(hid.astype(BF16), wd_ref[...], preferred_element_type=F32)

    @pl.when(f == 0)
    def _():
        xb_ref[...] = x_ref[...].astype(BF16)
        y_ref[...] = jnp.broadcast_to(bd_ref[...], y_ref.shape)

    @pl.when(used & (n_valid == tm))
    def _():
        for s in range(tm // grp):
            mlp(slice(s * grp, (s + 1) * grp))

    @pl.when(used & (n_valid < tm))
    def _():
        for sb in range(tm // sub):
            @pl.when(sb * sub < n_valid)
            def _(rows=slice(sb * sub, (sb + 1) * sub)):
                mlp(rows)


def _experts(xs, blk_exp, n_used, blk_valid, w_gu, b_gu, w_dn, b_dn, *, layer, tm, tf):
    P, D = xs.shape
    L, E, _, F2 = w_gu.shape
    FF = F2 // 2
    nf = FF // tf
    nblk = P // tm

    def blk(i, nu):
        return jnp.minimum(i, nu[0] - 1)

    def fidx(i, f, nu):
        return jnp.where(i < nu[0], f, nf - 1)

    def exp(i, be, nu):
        return be[blk(i, nu)]

    def spec(shape, fn):
        return pl.BlockSpec(shape, lambda i, f, be, nu, nv: fn(i, f, be, nu))

    return pl.pallas_call(
        functools.partial(_expert_kernel, tm=tm, sub=min(EXPERT_SUB, tm), grp=min(EXPERT_GROUP, tm)),
        grid_spec=pltpu.PrefetchScalarGridSpec(
            num_scalar_prefetch=3, grid=(nblk, nf),
            in_specs=[
                spec((tm, D), lambda i, f, be, nu: (blk(i, nu), 0)),
                spec((None, None, D, tf), lambda i, f, be, nu: (layer, exp(i, be, nu), 0, fidx(i, f, nu))),
                spec((None, None, D, tf), lambda i, f, be, nu: (layer, exp(i, be, nu), 0, nf + fidx(i, f, nu))),
                spec((None, None, 1, tf), lambda i, f, be, nu: (layer, exp(i, be, nu), 0, fidx(i, f, nu))),
                spec((None, None, 1, tf), lambda i, f, be, nu: (layer, exp(i, be, nu), 0, nf + fidx(i, f, nu))),
                spec((None, None, tf, D), lambda i, f, be, nu: (layer, exp(i, be, nu), fidx(i, f, nu), 0)),
                spec((None, None, 1, D), lambda i, f, be, nu: (layer, exp(i, be, nu), 0, 0))],
            out_specs=spec((tm, D), lambda i, f, be, nu: (i, 0)),
            scratch_shapes=[pltpu.VMEM((tm, D), BF16)]),
        out_shape=jax.ShapeDtypeStruct((P, D), F32),
        name="moe_experts",
        compiler_params=_cparams(("arbitrary", "arbitrary")),
    )(blk_exp, n_used, blk_valid, xs, w_gu, w_gu, b_gu.reshape(L, E, 1, F2), b_gu.reshape(L, E, 1, F2),
      w_dn, b_dn.reshape(L, E, 1, D))


def _combine_kernel(slot_ref, y_ref, tg_ref, xres_ref, g_ref, mod_ref, o_ref, buf_ref, sem, *, tm):
    base = pl.program_id(0) * tm * TOP_K

    def start(r, c):
        for k in range(TOP_K):
            _row_copy(y_ref, buf_ref.at[k], slot_ref[base + r * TOP_K + k], r, sem).start(priority=k % 2)
        return c

    def wait(r, c):
        for k in range(TOP_K):
            _row_copy(y_ref, buf_ref.at[k], slot_ref[base + r * TOP_K + k], r, sem).wait()
        return c

    lax.fori_loop(0, tm, start, 0, unroll=DMA_UNROLL)
    lax.fori_loop(0, tm, wait, 0, unroll=DMA_UNROLL)
    tg = tg_ref[...]
    f = tg[:, 0:1] * buf_ref[0]
    for k in range(1, TOP_K):
        f = f + tg[:, k:k + 1] * buf_ref[k]
    o_ref[...] = xres_ref[...] + mod_ref[5:6, :] * _rms(f, g_ref[3:4, :])


def _combine(slot, y, top_g, xres, g4, mod_l, *, tm, n_ctx, dec_seq):
    N, D = xres.shape
    return pl.pallas_call(
        functools.partial(_combine_kernel, tm=tm),
        grid_spec=pltpu.PrefetchScalarGridSpec(
            num_scalar_prefetch=1, grid=(N // tm,),
            in_specs=[pl.BlockSpec(memory_space=pl.ANY),
                      pl.BlockSpec((tm, TOP_K), lambda i, s: (i, 0)),
                      pl.BlockSpec((tm, D), lambda i, s: (i, 0)),
                      pl.BlockSpec((4, D), lambda i, s: (0, 0)),
                      pl.BlockSpec((None, 6, D), lambda i, s: (_mod_row(i, tm, n_ctx, dec_seq), 0, 0))],
            out_specs=pl.BlockSpec((tm, D), lambda i, s: (i, 0)),
            scratch_shapes=[pltpu.VMEM((TOP_K, tm, D), F32), pltpu.SemaphoreType.DMA]),
        out_shape=jax.ShapeDtypeStruct((N, D), F32),
        name="moe_combine",
        compiler_params=_cparams(("arbitrary",)),
    )(slot, y, top_g, xres, g4, mod_l)


def _moe(h, top_i, top_g, counts, xres, g4, mod_l, w_gu, b_gu, w_dn, b_dn, *, layer, tm_e, tf, tm_c,
         n_ctx, dec_seq):
    slot, blk_exp, n_used, blk_valid, pad_slot, n_pad = _routing(top_i, counts[:, 0].astype(jnp.int32), tm_e)
    top_g = top_g[:TOP_K].T
    xs = _dispatch(slot, pad_slot, n_pad, n_used, h, blk_exp.shape[0], tm=tm_c, tm_e=tm_e)
    y = _experts(xs, blk_exp, n_used, blk_valid, w_gu, b_gu, w_dn, b_dn, layer=layer, tm=tm_e, tf=tf)
    return _combine(slot, y, top_g, xres, g4, mod_l, tm=tm_c, n_ctx=n_ctx, dec_seq=dec_seq)


def kernel(x_prompt, x_sample, cache_k, cache_v, c, c_ctx, norm_g, w_ada, b_ada, w_qkv, w_attn_o,
           attn_sinks, w_conv_in, conv_w, w_conv_out, router_w, router_b, w_gate_up, b_gate_up,
           w_down, b_down):
    batch, seq, D = x_prompt.shape
    dec_batch, dec_seq, _ = x_sample.shape
    depth = norm_g.shape[0]
    n_ctx = batch * seq
    n_lat = dec_batch * dec_seq
    N = n_ctx + n_lat
    nq = N_HEADS * HEAD_DIM
    tm = 256
    tm_e = 512
    tf = min(1024, D)
    assert dec_batch + 1 <= MOD_ROWS and seq % tm == 0 and dec_seq % tm == 0 and n_ctx % tm_e == 0
    assert dec_seq % ATT_BLOCK == 0 and n_ctx % ATT_BLOCK == 0 and (N * TOP_K) % tm_e == 0

    x = jnp.concatenate([x_prompt.reshape(n_ctx, D), x_sample.reshape(n_lat, D)], axis=0)
    cond = jnp.zeros((MOD_ROWS, D), F32).at[0].set(c_ctx).at[1:1 + dec_batch].set(c)
    mod = _modulation(cond, w_ada, b_ada)
    rope_tabs = _rope_tables(dec_seq)
    w_gu_bf = w_gate_up.astype(BF16)
    w_dn_bf = w_down.astype(BF16)
    kw = dict(n_ctx=n_ctx, dec_seq=dec_seq)

    new_k, new_v = [], []
    for l in range(depth):
        g4 = norm_g[l]
        mod_l = mod[l]
        if l % 2 == 0:
            a = l // 2
            w = jnp.concatenate([w_qkv[a][:, :nq] * (HEAD_DIM ** -0.5), w_qkv[a][:, nq:]], axis=1)
            q, kh, vh, k32, v32 = _qkv_proj(x, g4[0:1], mod_l, w.astype(BF16), rope_tabs, tm=tm, **kw)
            new_k.append(k32[:n_ctx].reshape(batch, seq, N_KV_HEADS, HEAD_DIM))
            new_v.append(v32[:n_ctx].reshape(batch, seq, N_KV_HEADS, HEAD_DIM))
            ck = jnp.transpose(cache_k[:, a], (0, 2, 1, 3)).astype(BF16)
            cv = jnp.transpose(cache_v[:, a], (0, 2, 1, 3)).astype(BF16)
            o_ctx, o_lat = _attention(q, kh, vh, ck, cv, attn_sinks[a], batch=batch, seq=seq,
                                      dec_batch=dec_batch, dec_seq=dec_seq)
            w_o = w_attn_o[a].astype(BF16)
            x, h, top_i, top_g, counts = _attn_out(o_ctx, o_lat, w_o, x, g4, mod_l, router_w[l].T,
                                                   router_b[l][:, None], tm=tm, **kw)
        else:
            ci = l // 2
            bg, u = _conv_in(x, g4[0:1], mod_l, w_conv_in[ci].astype(BF16), tm=tm_e,
                             tc=min(512, D), **kw)
            x, h, top_i, top_g, counts = _conv_out(bg, u, conv_w[ci], w_conv_out[ci].astype(BF16), x, g4,
                                                   mod_l, router_w[l].T, router_b[l][:, None], tm=tm, seq=seq,
                                                   **kw)
        x = _moe(h, top_i, top_g, counts, x, g4, mod_l, w_gu_bf, b_gate_up, w_dn_bf, b_down,
                 layer=l, tm_e=tm_e, tf=tf, tm_c=tm, **kw)

    y_prompt = x[:n_ctx].reshape(batch, seq, D)
    y_sample = x[n_ctx:].reshape(dec_batch, dec_seq, D)
    return (y_prompt, y_sample, jnp.stack(new_k, axis=1), jnp.stack(new_v, axis=1))
```

```python
import functools

import jax
import jax.numpy as jnp
from jax import lax
from jax.experimental import pallas as pl
from jax.experimental.pallas import tpu as pltpu

N_HEADS = 32
N_KV_HEADS = 8
HEAD_DIM = 64
GROUP = N_HEADS // N_KV_HEADS
SLAB = GROUP * HEAD_DIM
ATT_BLOCK = 128
GRID_W = 64
ROPE_BASE = 10000.0
N_EXPERTS = 32
TOP_K = 4
SWIGLU_ALPHA = 1.702
SWIGLU_LIMIT = 7.0
RMS_EPS = 1e-6
NEG = -1e30
LANES = 128
MOD_ROWS = 8
DMA_UNROLL = 4
EXPERT_SUB = 128
ROUTE_GROUPS = 2
EXPERT_GROUP = 256

F32 = jnp.float32
BF16 = jnp.bfloat16

VMEM_LIMIT = 56 * 1024 * 1024


def _cparams(sem):
    return pltpu.CompilerParams(dimension_semantics=sem, vmem_limit_bytes=VMEM_LIMIT)


def _rms(x, g):
    return x * lax.rsqrt(jnp.mean(x * x, axis=-1, keepdims=True) + RMS_EPS) * g


def _mod_row(i, tm, n_ctx, dec_seq):
    r = i * tm
    return jnp.where(r < n_ctx, 0, 1 + (r - n_ctx) // dec_seq)


def _mod_kernel(cond_ref, w_ref, b_ref, o_ref):
    c = cond_ref[...]
    s = c * jax.nn.sigmoid(c)
    o_ref[...] = jnp.dot(s, w_ref[...], preferred_element_type=F32,
                         precision=lax.Precision.HIGHEST) + b_ref[...]


def _modulation(cond, w_ada, b_ada):
    L, D, D6 = w_ada.shape
    tn = 1024 if D6 % 1024 == 0 else D6
    out = pl.pallas_call(
        _mod_kernel,
        grid=(L, D6 // tn),
        in_specs=[pl.BlockSpec((MOD_ROWS, D), lambda l, j: (0, 0)),
                  pl.BlockSpec((None, D, tn), lambda l, j: (l, 0, j)),
                  pl.BlockSpec((None, 1, tn), lambda l, j: (l, 0, j))],
        out_specs=pl.BlockSpec((None, MOD_ROWS, tn), lambda l, j: (l, 0, j)),
        out_shape=jax.ShapeDtypeStruct((L, MOD_ROWS, D6), F32),
        name="adaln_mod",
        compiler_params=_cparams(("arbitrary", "arbitrary")),
    )(cond, w_ada, b_ada.reshape(L, 1, D6))
    return out.reshape(L, MOD_ROWS, 6, D)


def _qkv_kernel(x_ref, g_ref, mod_ref, w_ref, c_ref, s1_ref, s2_ref,
                q_ref, kh_ref, vh_ref, k32_ref, v32_ref, *, tm, n_ctx):
    i = pl.program_id(0)
    h = _rms(x_ref[...], g_ref[...]) * (1 + mod_ref[1:2, :]) + mod_ref[0:1, :]
    qkv = jnp.dot(h.astype(BF16), w_ref[...], preferred_element_type=F32)
    is_latent = i * tm >= n_ctx
    cos, sin_lo, sin_hi = c_ref[...], s1_ref[...], s2_ref[...]

    def rope(ch):
        r = ch * cos + pltpu.roll(ch, LANES - 16, 1) * sin_lo + pltpu.roll(ch, 16, 1) * sin_hi
        return jnp.where(is_latent, r, ch)

    nq = N_HEADS * HEAD_DIM
    nkv = N_KV_HEADS * HEAD_DIM
    for kh in range(N_KV_HEADS):
        parts = [rope(qkv[:, kh * SLAB + c * LANES: kh * SLAB + (c + 1) * LANES])
                 for c in range(SLAB // LANES)]
        q_ref[kh] = jnp.concatenate(parts, axis=1).astype(BF16)
    for c in range(nkv // LANES):
        kc = rope(qkv[:, nq + c * LANES: nq + (c + 1) * LANES])
        vc = qkv[:, nq + nkv + c * LANES: nq + nkv + (c + 1) * LANES]
        k32_ref[:, c * LANES:(c + 1) * LANES] = kc
        v32_ref[:, c * LANES:(c + 1) * LANES] = vc
        for s in range(LANES // HEAD_DIM):
            kh_ref[2 * c + s] = kc[:, s * HEAD_DIM:(s + 1) * HEAD_DIM].astype(BF16)
            vh_ref[2 * c + s] = vc[:, s * HEAD_DIM:(s + 1) * HEAD_DIM].astype(BF16)


def _qkv_proj(x, g, mod_l, w_bf, rope_tabs, *, tm, n_ctx, dec_seq):
    N, D = x.shape
    nq = N_HEADS * HEAD_DIM
    nkv = N_KV_HEADS * HEAD_DIM
    cos, sin_lo, sin_hi = rope_tabs
    tiles_per_seq = dec_seq // tm

    def tab_idx(i):
        t = jnp.maximum(i * tm - n_ctx, 0) // tm
        return (t % tiles_per_seq, 0)

    row = lambda i: (i, 0)
    return pl.pallas_call(
        functools.partial(_qkv_kernel, tm=tm, n_ctx=n_ctx),
        grid=(N // tm,),
        in_specs=[pl.BlockSpec((tm, D), row),
                  pl.BlockSpec((1, D), lambda i: (0, 0)),
                  pl.BlockSpec((None, 6, D), lambda i: (_mod_row(i, tm, n_ctx, dec_seq), 0, 0)),
                  pl.BlockSpec((D, nq + 2 * nkv), lambda i: (0, 0)),
                  pl.BlockSpec((tm, LANES), tab_idx),
                  pl.BlockSpec((tm, LANES), tab_idx),
                  pl.BlockSpec((tm, LANES), tab_idx)],
        out_specs=[pl.BlockSpec((N_KV_HEADS, tm, SLAB), lambda i: (0, i, 0)),
                   pl.BlockSpec((N_KV_HEADS, tm, HEAD_DIM), lambda i: (0, i, 0)),
                   pl.BlockSpec((N_KV_HEADS, tm, HEAD_DIM), lambda i: (0, i, 0)),
                   pl.BlockSpec((tm, nkv), row),
                   pl.BlockSpec((tm, nkv), row)],
        out_shape=[jax.ShapeDtypeStruct((N_KV_HEADS, N, SLAB), BF16),
                   jax.ShapeDtypeStruct((N_KV_HEADS, N, HEAD_DIM), BF16),
                   jax.ShapeDtypeStruct((N_KV_HEADS, N, HEAD_DIM), BF16),
                   jax.ShapeDtypeStruct((N, nkv), F32),
                   jax.ShapeDtypeStruct((N, nkv), F32)],
        name="qkv_proj",
        compiler_params=_cparams(("arbitrary",)),
    )(x, g, mod_l, w_bf, cos, sin_lo, sin_hi)


def _rope_tables(dec_seq):
    half = HEAD_DIM // 4
    inv = ROPE_BASE ** (-jnp.arange(half, dtype=F32) / half)
    pos = jnp.arange(dec_seq, dtype=jnp.int32)
    row = (pos // GRID_W).astype(F32)[:, None] * inv
    col = (pos % GRID_W).astype(F32)[:, None] * inv
    z = jnp.zeros_like(row)
    cos = jnp.concatenate([jnp.cos(row), jnp.cos(row), jnp.cos(col), jnp.cos(col)], axis=1)
    sin_lo = jnp.concatenate([-jnp.sin(row), z, -jnp.sin(col), z], axis=1)
    sin_hi = jnp.concatenate([z, jnp.sin(row), z, jnp.sin(col)], axis=1)
    rep = LANES // HEAD_DIM
    return tuple(jnp.tile(t, (1, rep)) for t in (cos, sin_lo, sin_hi))


def _attn_heads(q, k, v, mask, sink_ref, kh):
    T = q.shape[0]
    qs = jnp.concatenate([q[:, g * HEAD_DIM:(g + 1) * HEAD_DIM] for g in range(GROUP)], axis=0)
    s = lax.dot_general(qs, k, (((1,), (1,)), ((), ())), preferred_element_type=F32)
    if mask is not None:
        s = jnp.where(mask, s, NEG)
    head = lax.broadcasted_iota(jnp.int32, (GROUP * T, 1), 0) // T
    sink = jnp.zeros((GROUP * T, 1), F32)
    for g in range(GROUP):
        sink = jnp.where(head == g, sink_ref[kh * GROUP + g], sink)
    m = jnp.maximum(jnp.max(s, axis=-1, keepdims=True), sink)
    p = jnp.exp(s - m)
    den = jnp.sum(p, axis=-1, keepdims=True) + jnp.exp(sink - m)
    o = jnp.dot(p.astype(BF16), v, preferred_element_type=F32) / den
    return jnp.concatenate([o[g * T:(g + 1) * T] for g in range(GROUP)], axis=1)


def _ctx_attn_kernel(sink_ref, q_ref, k_ref, v_ref, o_ref):
    def body(kh, carry):
        o_ref[kh] = _attn_heads(q_ref[kh], k_ref[kh], v_ref[kh], None, sink_ref, kh).astype(BF16)
        return carry

    lax.fori_loop(0, N_KV_HEADS, body, 0, unroll=4)


def _lat_attn_kernel(sink_ref, q_ref, kp_ref, kc_ref, kn_ref, vp_ref, vc_ref, vn_ref,
                     ck_ref, cv_ref, o_ref, *, nb, past):
    i = pl.program_id(1)
    T = ATT_BLOCK
    shape = (GROUP * T, 3 * T + past)
    a = lax.broadcasted_iota(jnp.int32, shape, 0) % T
    col = lax.broadcasted_iota(jnp.int32, shape, 1)
    seg = col // T
    c = col % T
    mask = (((seg == 0) & (c >= a) & (i > 0)) | (seg == 1)
            | ((seg == 2) & (c <= a) & (i < nb - 1)) | (seg >= 3))

    def body(kh, carry):
        k = jnp.concatenate([kp_ref[kh], kc_ref[kh], kn_ref[kh], ck_ref[kh]], axis=0)
        v = jnp.concatenate([vp_ref[kh], vc_ref[kh], vn_ref[kh], cv_ref[kh]], axis=0)
        o_ref[kh] = _attn_heads(q_ref[kh], k, v, mask, sink_ref, kh).astype(BF16)
        return carry

    lax.fori_loop(0, N_KV_HEADS, body, 0, unroll=4)


def _attention(q, kh, vh, ck, cv, sinks, *, batch, seq, dec_batch, dec_seq):
    N = q.shape[1]
    n_ctx = batch * seq
    past = ck.shape[2]
    smem = pl.BlockSpec(memory_space=pltpu.SMEM)
    o_ctx = pl.pallas_call(
        _ctx_attn_kernel,
        grid=(batch,),
        in_specs=[smem,
                  pl.BlockSpec((N_KV_HEADS, seq, SLAB), lambda n: (0, n, 0)),
                  pl.BlockSpec((N_KV_HEADS, seq, HEAD_DIM), lambda n: (0, n, 0)),
                  pl.BlockSpec((N_KV_HEADS, seq, HEAD_DIM), lambda n: (0, n, 0))],
        out_specs=pl.BlockSpec((N_KV_HEADS, seq, SLAB), lambda n: (0, n, 0)),
        out_shape=jax.ShapeDtypeStruct((N_KV_HEADS, n_ctx, SLAB), BF16),
        name="ctx_attn",
        compiler_params=_cparams(("arbitrary",)),
    )(sinks, q, kh, vh)

    T = ATT_BLOCK
    nb = dec_seq // T
    base = n_ctx // T
    cur = lambda b, i: (0, base + b * nb + i, 0)
    prv = lambda b, i: (0, base + b * nb + jnp.maximum(i - 1, 0), 0)
    nxt = lambda b, i: (0, base + b * nb + jnp.minimum(i + 1, nb - 1), 0)
    kv_spec = lambda f: pl.BlockSpec((N_KV_HEADS, T, HEAD_DIM), f)
    ctx_spec = pl.BlockSpec((None, N_KV_HEADS, past, HEAD_DIM), lambda b, i: (b, 0, 0, 0))
    o_lat = pl.pallas_call(
        functools.partial(_lat_attn_kernel, nb=nb, past=past),
        grid=(dec_batch, nb),
        in_specs=[smem,
                  pl.BlockSpec((N_KV_HEADS, T, SLAB), cur),
                  kv_spec(prv), kv_spec(cur), kv_spec(nxt),
                  kv_spec(prv), kv_spec(cur), kv_spec(nxt),
                  ctx_spec, ctx_spec],
        out_specs=pl.BlockSpec((N_KV_HEADS, T, SLAB), lambda b, i: (0, b * nb + i, 0)),
        out_shape=jax.ShapeDtypeStruct((N_KV_HEADS, N - n_ctx, SLAB), BF16),
        name="lat_attn",
        compiler_params=_cparams(("arbitrary", "arbitrary")),
    )(sinks, q, kh, kh, kh, vh, vh, vh, ck, cv)
    return o_ctx, o_lat


def _conv_in_kernel(x_ref, g_ref, mod_ref, wb_ref, wc_ref, wx_ref, bg_ref, u_ref, h_ref):
    @pl.when(pl.program_id(1) == 0)
    def _():
        h = _rms(x_ref[...], g_ref[...]) * (1 + mod_ref[1:2, :]) + mod_ref[0:1, :]
        h_ref[...] = h.astype(BF16)

    h = h_ref[...]
    bg_ref[...] = jnp.dot(h, wb_ref[...], preferred_element_type=F32)
    cg = jnp.dot(h, wc_ref[...], preferred_element_type=F32)
    xin = jnp.dot(h, wx_ref[...], preferred_element_type=F32)
    u_ref[...] = cg * xin


def _conv_in(x, g, mod_l, w_bf, *, tm, tc, n_ctx, dec_seq):
    N, D = x.shape
    nc = D // tc
    wspec = lambda k: pl.BlockSpec((D, tc), lambda i, j: (0, k * nc + j))
    return pl.pallas_call(
        _conv_in_kernel,
        grid=(N // tm, nc),
        in_specs=[pl.BlockSpec((tm, D), lambda i, j: (i, 0)),
                  pl.BlockSpec((1, D), lambda i, j: (0, 0)),
                  pl.BlockSpec((None, 6, D), lambda i, j: (_mod_row(i, tm, n_ctx, dec_seq), 0, 0)),
                  wspec(0), wspec(1), wspec(2)],
        out_specs=[pl.BlockSpec((tm, tc), lambda i, j: (i, j)),
                   pl.BlockSpec((tm, tc), lambda i, j: (i, j))],
        out_shape=[jax.ShapeDtypeStruct((N, D), F32), jax.ShapeDtypeStruct((N, D), F32)],
        scratch_shapes=[pltpu.VMEM((tm, D), BF16)],
        name="conv_in",
        compiler_params=_cparams(("arbitrary", "arbitrary")),
    )(x, g, mod_l, w_bf, w_bf, w_bf)


def _route_rows(proj, xres, g_ref, mod_ref, rwt_ref, rb_ref, carry):
    y = xres + mod_ref[2:3, :] * _rms(proj, g_ref[1:2, :])
    h = _rms(y, g_ref[2:3, :]) * (1 + mod_ref[4:5, :]) + mod_ref[3:4, :]
    logits = lax.dot_general(rwt_ref[...], h, (((1,), (1,)), ((), ())), preferred_element_type=F32,
                             precision=lax.Precision.HIGHEST) + rb_ref[...]
    tm = logits.shape[1]
    row_e = lax.broadcasted_iota(jnp.int32, logits.shape, 0)
    rest = logits
    first = None
    den = None
    hits, ids, probs = [], [], []
    for k in range(TOP_K):
        m = jnp.max(rest, axis=0, keepdims=True)
        idx = jnp.min(jnp.where(rest == m, row_e, N_EXPERTS), axis=0, keepdims=True)
        hit = row_e == idx
        rest = jnp.where(hit, -jnp.inf, rest)
        if k == 0:
            first = m
        p = jnp.exp(m - first)
        den = p if k == 0 else den + p
        hits.append(hit)
        ids.append(idx)
        probs.append(p)
    zero = jnp.zeros((1, tm), F32)
    tg = jnp.concatenate([p / den for p in probs] + [zero] * (8 - TOP_K), axis=0)
    onehot = jnp.zeros(logits.shape, F32)
    for hit in hits:
        onehot = onehot + hit.astype(F32)
    earlier = (lax.broadcasted_iota(jnp.int32, (tm, tm), 0)
               < lax.broadcasted_iota(jnp.int32, (tm, tm), 1)).astype(BF16)
    before = jnp.dot(onehot.astype(BF16), earlier, preferred_element_type=F32) + carry
    ranks = [jnp.sum(jnp.where(hit, before, 0.0), axis=0, keepdims=True).astype(jnp.int32) for hit in hits]
    ti = jnp.concatenate(ids + ranks, axis=0)
    return y, h, ti, tg, carry + jnp.sum(onehot, axis=1, keepdims=True)


def _post_epilogue(proj, xres_ref, g_ref, mod_ref, rwt_ref, rb_ref, y_ref, h_ref, ti_ref, tg_ref, cnt_ref):
    @pl.when(pl.program_id(0) == 0)
    def _():
        cnt_ref[...] = jnp.zeros_like(cnt_ref)

    tm = proj.shape[0]
    grp = tm // ROUTE_GROUPS
    carry = cnt_ref[:, 0:1]
    for s in range(ROUTE_GROUPS):
        rows = slice(s * grp, (s + 1) * grp)
        y, h, ti, tg, carry = _route_rows(proj[rows], xres_ref[rows, :], g_ref, mod_ref, rwt_ref, rb_ref, carry)
        y_ref[rows, :] = y
        h_ref[rows, :] = h
        ti_ref[:, rows] = ti
        tg_ref[:, rows] = tg
    cnt_ref[...] = jnp.broadcast_to(carry, cnt_ref.shape)


def _attn_out_kernel(oc_ref, ol_ref, w_ref, xres_ref, g_ref, mod_ref, rw_ref, rb_ref,
                     y_ref, h_ref, ti_ref, tg_ref, cnt_ref, *, tm, n_ctx):
    is_latent = pl.program_id(0) * tm >= n_ctx
    o = jnp.concatenate([jnp.where(is_latent, ol_ref[kh], oc_ref[kh]) for kh in range(N_KV_HEADS)], axis=1)
    proj = jnp.dot(o, w_ref[...], preferred_element_type=F32)
    _post_epilogue(proj, xres_ref, g_ref, mod_ref, rw_ref, rb_ref, y_ref, h_ref, ti_ref, tg_ref, cnt_ref)


def _conv_out_kernel(bg_ref, u_ref, up_ref, un_ref, cw_ref, w_ref, xres_ref, g_ref, mod_ref,
                     rw_ref, rb_ref, y_ref, h_ref, ti_ref, tg_ref, cnt_ref, *, tm, n_ctx, seq, dec_seq):
    i = pl.program_id(0)
    u = u_ref[...]
    row = lax.broadcasted_iota(jnp.int32, (tm, 1), 0)
    r = i * tm + row
    pos = jnp.where(r < n_ctx, r % seq, (r - n_ctx) % dec_seq)
    last = jnp.where(r < n_ctx, seq - 1, dec_seq - 1)
    prev = jnp.where(row == 0, up_ref[7:8, :], pltpu.roll(u, 1, 0))
    prev = jnp.where(pos == 0, 0.0, prev)
    nxt = jnp.where(row == tm - 1, un_ref[0:1, :], pltpu.roll(u, tm - 1, 0))
    nxt = jnp.where(pos == last, 0.0, nxt)
    conv = prev * cw_ref[0:1, :] + u * cw_ref[1:2, :] + nxt * cw_ref[2:3, :]
    z = (bg_ref[...] * conv).astype(BF16)
    proj = jnp.dot(z, w_ref[...], preferred_element_type=F32)
    _post_epilogue(proj, xres_ref, g_ref, mod_ref, rw_ref, rb_ref, y_ref, h_ref, ti_ref, tg_ref, cnt_ref)


def _mixer_out(name, kernel, lead_args, lead_specs, xres, g4, mod_l, router_w, router_b, *, tm, n_ctx,
               dec_seq):
    N, D = xres.shape
    row = lambda i: (i, 0)
    const = lambda i: (0, 0)
    return pl.pallas_call(
        kernel,
        grid=(N // tm,),
        in_specs=lead_specs + [
            pl.BlockSpec((tm, D), row),
            pl.BlockSpec((4, D), const),
            pl.BlockSpec((None, 6, D), lambda i: (_mod_row(i, tm, n_ctx, dec_seq), 0, 0)),
            pl.BlockSpec((N_EXPERTS, D), const),
            pl.BlockSpec((N_EXPERTS, 1), const)],
        out_specs=[pl.BlockSpec((tm, D), row), pl.BlockSpec((tm, D), row),
                   pl.BlockSpec((8, tm), lambda i: (0, i)), pl.BlockSpec((8, tm), lambda i: (0, i)),
                   pl.BlockSpec((N_EXPERTS, LANES), const)],
        out_shape=[jax.ShapeDtypeStruct((N, D), F32), jax.ShapeDtypeStruct((N, D), F32),
                   jax.ShapeDtypeStruct((8, N), jnp.int32),
                   jax.ShapeDtypeStruct((8, N), F32),
                   jax.ShapeDtypeStruct((N_EXPERTS, LANES), F32)],
        name=name,
        compiler_params=_cparams(("arbitrary",)),
    )(*lead_args, xres, g4, mod_l, router_w, router_b)


def _attn_out(o_ctx, o_lat, w_bf, xres, g4, mod_l, router_w, router_b, *, tm, n_ctx, dec_seq):
    D = xres.shape[1]
    tc = n_ctx // tm
    specs = [pl.BlockSpec((N_KV_HEADS, tm, SLAB), lambda i: (0, jnp.minimum(i, tc - 1), 0)),
             pl.BlockSpec((N_KV_HEADS, tm, SLAB), lambda i: (0, jnp.maximum(i - tc, 0), 0)),
             pl.BlockSpec((N_KV_HEADS * SLAB, D), lambda i: (0, 0))]
    kern = functools.partial(_attn_out_kernel, tm=tm, n_ctx=n_ctx)
    return _mixer_out("attn_out", kern, [o_ctx, o_lat, w_bf], specs, xres, g4, mod_l, router_w, router_b,
                      tm=tm, n_ctx=n_ctx, dec_seq=dec_seq)


def _conv_out(bg, u, conv_w, w_bf, xres, g4, mod_l, router_w, router_b, *, tm, n_ctx, seq, dec_seq):
    N, D = xres.shape
    sub = 8
    nsub = N // sub
    specs = [pl.BlockSpec((tm, D), lambda i: (i, 0)),
             pl.BlockSpec((tm, D), lambda i: (i, 0)),
             pl.BlockSpec((sub, D), lambda i: (jnp.maximum(i * (tm // sub) - 1, 0), 0)),
             pl.BlockSpec((sub, D), lambda i: (jnp.minimum((i + 1) * (tm // sub), nsub - 1), 0)),
             pl.BlockSpec((3, D), lambda i: (0, 0)),
             pl.BlockSpec((D, D), lambda i: (0, 0))]
    kern = functools.partial(_conv_out_kernel, tm=tm, n_ctx=n_ctx, seq=seq, dec_seq=dec_seq)
    return _mixer_out("conv_out", kern, [bg, u, u, u, conv_w, w_bf], specs, xres, g4, mod_l, router_w, router_b,
                      tm=tm, n_ctx=n_ctx, dec_seq=dec_seq)


def _routing(top_i, counts, tm):
    N = top_i.shape[1]
    nblk = (N * TOP_K) // tm + N_EXPERTS
    top_e = top_i[:TOP_K].T
    rank = top_i[TOP_K:2 * TOP_K].T
    blocks = (counts + tm - 1) // tm
    blk_end = jnp.cumsum(blocks)
    pad_start = (blk_end - blocks) * tm
    experts = jnp.arange(N_EXPERTS, dtype=jnp.int32)
    start = jnp.sum(jnp.where(top_e[:, :, None] == experts, pad_start, 0), axis=-1)
    slot = (start + rank).reshape(N * TOP_K)
    blk = jnp.arange(nblk, dtype=jnp.int32)
    blk_exp = jnp.minimum(jnp.sum((blk[:, None] >= blk_end[None, :]).astype(jnp.int32), axis=1),
                          N_EXPERTS - 1)
    first_blk = blk_end - blocks
    of_blk = lambda v: jnp.sum(jnp.where(blk_exp[:, None] == experts, v, 0), axis=-1)
    blk_valid = jnp.clip(of_blk(counts) - (blk - of_blk(first_blk)) * tm, 0, tm)
    pad_len = blocks * tm - counts
    pad_end = jnp.cumsum(pad_len)
    p = jnp.arange(N_EXPERTS * (tm - 1), dtype=jnp.int32)
    owner = jnp.minimum(jnp.sum((p[:, None] >= pad_end[None, :]).astype(jnp.int32), axis=1), N_EXPERTS - 1)
    pick = lambda v: jnp.sum(jnp.where(owner[:, None] == experts, v, 0), axis=-1)
    pad_slot = pick(pad_start + counts) + p - pick(pad_end - pad_len)
    pad_slot = jnp.where(p < pad_end[-1], pad_slot, 0)
    return slot, blk_exp, blk_end[-1:], blk_valid, pad_slot, pad_end[-1:]


def _row_copy(src_ref, dst_ref, s, d, sem):
    return pltpu.make_async_copy(src_ref.at[pl.ds(s, 1), :], dst_ref.at[pl.ds(d, 1), :], sem)


def _dispatch_kernel(slot_ref, pad_ref, npad_ref, nused_ref, h_ref, xs_ref, zero_ref, sem, zsem, *,
                     tm, tm_e, nblk):
    base = pl.program_id(0) * tm * TOP_K

    def start(r, c):
        for k in range(TOP_K):
            _row_copy(h_ref, xs_ref, r, slot_ref[base + r * TOP_K + k], sem).start(priority=k % 2)
        return c

    def wait(r, c):
        for k in range(TOP_K):
            _row_copy(h_ref, xs_ref, r, slot_ref[base + r * TOP_K + k], sem).wait()
        return c

    lax.fori_loop(0, tm, start, 0, unroll=DMA_UNROLL)

    @pl.when(pl.program_id(0) == pl.num_programs(0) - 1)
    def _():
        zero_ref[...] = jnp.zeros_like(zero_ref)
        n_pad = npad_ref[0]
        n_used = nused_ref[0]

        def block_copy(b):
            dst = xs_ref.at[pl.ds(pl.multiple_of(b * tm_e, tm_e), tm_e), :]
            return pltpu.make_async_copy(zero_ref, dst, zsem)

        def row_start(p, c):
            _row_copy(zero_ref, xs_ref, 0, pad_ref[p], zsem).start()
            return c

        def row_wait(p, c):
            _row_copy(zero_ref, xs_ref, 0, pad_ref[p], zsem).wait()
            return c

        def blk_start(b, c):
            block_copy(b).start()
            return c

        def blk_wait(b, c):
            block_copy(b).wait()
            return c

        lax.fori_loop(0, n_pad, row_start, 0)
        lax.fori_loop(n_used, nblk, blk_start, 0)
        lax.fori_loop(0, n_pad, row_wait, 0)
        lax.fori_loop(n_used, nblk, blk_wait, 0)

    lax.fori_loop(0, tm, wait, 0, unroll=DMA_UNROLL)


def _dispatch(slot, pad_slot, n_pad, n_used, h, nblk, *, tm, tm_e):
    N, D = h.shape
    return pl.pallas_call(
        functools.partial(_dispatch_kernel, tm=tm, tm_e=tm_e, nblk=nblk),
        grid_spec=pltpu.PrefetchScalarGridSpec(
            num_scalar_prefetch=4, grid=(N // tm,),
            in_specs=[pl.BlockSpec((tm, D), lambda i, *_: (i, 0))],
            out_specs=pl.BlockSpec(memory_space=pl.ANY),
            scratch_shapes=[pltpu.VMEM((tm_e, D), h.dtype), pltpu.SemaphoreType.DMA,
                            pltpu.SemaphoreType.DMA]),
        out_shape=jax.ShapeDtypeStruct((nblk * tm_e, D), h.dtype),
        name="moe_dispatch",
        compiler_params=_cparams(("arbitrary",)),
    )(slot, pad_slot, n_pad, n_used, h)


def _expert_kernel(be_ref, nu_ref, nv_ref, x_ref, wg_ref, wu_ref, bgt_ref, bup_ref, wd_ref, bd_ref,
                   y_ref, xb_ref, *, tm, sub, grp):
    i = pl.program_id(0)
    f = pl.program_id(1)
    used = i < nu_ref[0]
    n_valid = nv_ref[i]

    def mlp(rows):
        xb = xb_ref[rows, :]
        gate = jnp.dot(xb, wg_ref[...], preferred_element_type=F32) + bgt_ref[...]
        up = jnp.dot(xb, wu_ref[...], preferred_element_type=F32) + bup_ref[...]
        gate = jnp.minimum(gate, SWIGLU_LIMIT)
        up = jnp.clip(up, -SWIGLU_LIMIT, SWIGLU_LIMIT)
        hid = (up + 1) * (gate * jax.nn.sigmoid(SWIGLU_ALPHA * gate))
        y_ref[rows, :] += jnp.dot(hid.astype(BF16), wd_ref[...], preferred_element_type=F32)

    @pl.when(f == 0)
    def _():
        xb_ref[...] = x_ref[...].astype(BF16)
        y_ref[...] = jnp.broadcast_to(bd_ref[...], y_ref.shape)

    @pl.when(used & (n_valid == tm))
    def _():
        for s in range(tm // grp):
            mlp(slice(s * grp, (s + 1) * grp))

    @pl.when(used & (n_valid < tm))
    def _():
        for sb in range(tm // sub):
            @pl.when(sb * sub < n_valid)
            def _(rows=slice(sb * sub, (sb + 1) * sub)):
                mlp(rows)


def _experts(xs, blk_exp, n_used, blk_valid, w_gu, b_gu, w_dn, b_dn, *, layer, tm, tf):
    P, D = xs.shape
    L, E, _, F2 = w_gu.shape
    FF = F2 // 2
    nf = FF // tf
    nblk = P // tm

    def blk(i, nu):
        return jnp.minimum(i, nu[0] - 1)

    def fidx(i, f, nu):
        return jnp.where(i < nu[0], f, nf - 1)

    def exp(i, be, nu):
        return be[blk(i, nu)]

    def spec(shape, fn):
        return pl.BlockSpec(shape, lambda i, f, be, nu, nv: fn(i, f, be, nu))

    return pl.pallas_call(
        functools.partial(_expert_kernel, tm=tm, sub=min(EXPERT_SUB, tm), grp=min(EXPERT_GROUP, tm)),
        grid_spec=pltpu.PrefetchScalarGridSpec(
            num_scalar_prefetch=3, grid=(nblk, nf),
            in_specs=[
                spec((tm, D), lambda i, f, be, nu: (blk(i, nu), 0)),
                spec((None, None, D, tf), lambda i, f, be, nu: (layer, exp(i, be, nu), 0, fidx(i, f, nu))),
                spec((None, None, D, tf), lambda i, f, be, nu: (layer, exp(i, be, nu), 0, nf + fidx(i, f, nu))),
                spec((None, None, 1, tf), lambda i, f, be, nu: (layer, exp(i, be, nu), 0, fidx(i, f, nu))),
                spec((None, None, 1, tf), lambda i, f, be, nu: (layer, exp(i, be, nu), 0, nf + fidx(i, f, nu))),
                spec((None, None, tf, D), lambda i, f, be, nu: (layer, exp(i, be, nu), fidx(i, f, nu), 0)),
                spec((None, None, 1, D), lambda i, f, be, nu: (layer, exp(i, be, nu), 0, 0))],
            out_specs=spec((tm, D), lambda i, f, be, nu: (i, 0)),
            scratch_shapes=[pltpu.VMEM((tm, D), BF16)]),
        out_shape=jax.ShapeDtypeStruct((P, D), F32),
        name="moe_experts",
        compiler_params=_cparams(("arbitrary", "arbitrary")),
    )(blk_exp, n_used, blk_valid, xs, w_gu, w_gu, b_gu.reshape(L, E, 1, F2), b_gu.reshape(L, E, 1, F2),
      w_dn, b_dn.reshape(L, E, 1, D))


def _combine_kernel(slot_ref, y_ref, tg_ref, xres_ref, g_ref, mod_ref, o_ref, buf_ref, sem, *, tm):
    base = pl.program_id(0) * tm * TOP_K

    def start(r, c):
        for k in range(TOP_K):
            _row_copy(y_ref, buf_ref.at[k], slot_ref[base + r * TOP_K + k], r, sem).start(priority=k % 2)
        return c

    def wait(r, c):
        for k in range(TOP_K):
            _row_copy(y_ref, buf_ref.at[k], slot_ref[base + r * TOP_K + k], r, sem).wait()
        return c

    lax.fori_loop(0, tm, start, 0, unroll=DMA_UNROLL)
    lax.fori_loop(0, tm, wait, 0, unroll=DMA_UNROLL)
    tg = tg_ref[...]
    f = tg[:, 0:1] * buf_ref[0]
    for k in range(1, TOP_K):
        f = f + tg[:, k:k + 1] * buf_ref[k]
    o_ref[...] = xres_ref[...] + mod_ref[5:6, :] * _rms(f, g_ref[3:4, :])


def _combine(slot, y, top_g, xres, g4, mod_l, *, tm, n_ctx, dec_seq):
    N, D = xres.shape
    return pl.pallas_call(
        functools.partial(_combine_kernel, tm=tm),
        grid_spec=pltpu.PrefetchScalarGridSpec(
            num_scalar_prefetch=1, grid=(N // tm,),
            in_specs=[pl.BlockSpec(memory_space=pl.ANY),
                      pl.BlockSpec((tm, TOP_K), lambda i, s: (i, 0)),
                      pl.BlockSpec((tm, D), lambda i, s: (i, 0)),
                      pl.BlockSpec((4, D), lambda i, s: (0, 0)),
                      pl.BlockSpec((None, 6, D), lambda i, s: (_mod_row(i, tm, n_ctx, dec_seq), 0, 0))],
            out_specs=pl.BlockSpec((tm, D), lambda i, s: (i, 0)),
            scratch_shapes=[pltpu.VMEM((TOP_K, tm, D), F32), pltpu.SemaphoreType.DMA]),
        out_shape=jax.ShapeDtypeStruct((N, D), F32),
        name="moe_combine",
        compiler_params=_cparams(("arbitrary",)),
    )(slot, y, top_g, xres, g4, mod_l)


def _moe(h, top_i, top_g, counts, xres, g4, mod_l, w_gu, b_gu, w_dn, b_dn, *, layer, tm_e, tf, tm_c,
         n_ctx, dec_seq):
    slot, blk_exp, n_used, blk_valid, pad_slot, n_pad = _routing(top_i, counts[:, 0].astype(jnp.int32), tm_e)
    top_g = top_g[:TOP_K].T
    xs = _dispatch(slot, pad_slot, n_pad, n_used, h, blk_exp.shape[0], tm=tm_c, tm_e=tm_e)
    y = _experts(xs, blk_exp, n_used, blk_valid, w_gu, b_gu, w_dn, b_dn, layer=layer, tm=tm_e, tf=tf)
    return _combine(slot, y, top_g, xres, g4, mod_l, tm=tm_c, n_ctx=n_ctx, dec_seq=dec_seq)


def kernel(x_prompt, x_sample, cache_k, cache_v, c, c_ctx, norm_g, w_ada, b_ada, w_qkv, w_attn_o,
           attn_sinks, w_conv_in, conv_w, w_conv_out, router_w, router_b, w_gate_up, b_gate_up,
           w_down, b_down):
    batch, seq, D = x_prompt.shape
    dec_batch, dec_seq, _ = x_sample.shape
    depth = norm_g.shape[0]
    n_ctx = batch * seq
    n_lat = dec_batch * dec_seq
    N = n_ctx + n_lat
    nq = N_HEADS * HEAD_DIM
    tm = 256
    tm_ci = 512
    tm_e = 1024
    tf = min(512, D)
    assert dec_batch + 1 <= MOD_ROWS and seq % tm == 0 and dec_seq % tm == 0 and n_ctx % tm_ci == 0
    assert dec_seq % ATT_BLOCK == 0 and n_ctx % ATT_BLOCK == 0 and (N * TOP_K) % tm_e == 0

    x = jnp.concatenate([x_prompt.reshape(n_ctx, D), x_sample.reshape(n_lat, D)], axis=0)
    cond = jnp.zeros((MOD_ROWS, D), F32).at[0].set(c_ctx).at[1:1 + dec_batch].set(c)
    mod = _modulation(cond, w_ada, b_ada)
    rope_tabs = _rope_tables(dec_seq)
    w_gu_bf = w_gate_up.astype(BF16)
    w_dn_bf = w_down.astype(BF16)
    kw = dict(n_ctx=n_ctx, dec_seq=dec_seq)

    new_k, new_v = [], []
    for l in range(depth):
        g4 = norm_g[l]
        mod_l = mod[l]
        if l % 2 == 0:
            a = l // 2
            w = jnp.concatenate([w_qkv[a][:, :nq] * (HEAD_DIM ** -0.5), w_qkv[a][:, nq:]], axis=1)
            q, kh, vh, k32, v32 = _qkv_proj(x, g4[0:1], mod_l, w.astype(BF16), rope_tabs, tm=tm, **kw)
            new_k.append(k32[:n_ctx].reshape(batch, seq, N_KV_HEADS, HEAD_DIM))
            new_v.append(v32[:n_ctx].reshape(batch, seq, N_KV_HEADS, HEAD_DIM))
            ck = jnp.transpose(cache_k[:, a], (0, 2, 1, 3)).astype(BF16)
            cv = jnp.transpose(cache_v[:, a], (0, 2, 1, 3)).astype(BF16)
            o_ctx, o_lat = _attention(q, kh, vh, ck, cv, attn_sinks[a], batch=batch, seq=seq,
                                      dec_batch=dec_batch, dec_seq=dec_seq)
            w_o = w_attn_o[a].astype(BF16)
            x, h, top_i, top_g, counts = _attn_out(o_ctx, o_lat, w_o, x, g4, mod_l, router_w[l].T,
                                                   router_b[l][:, None], tm=tm, **kw)
        else:
            ci = l // 2
            bg, u = _conv_in(x, g4[0:1], mod_l, w_conv_in[ci].astype(BF16), tm=tm_ci,
                             tc=min(512, D), **kw)
            x, h, top_i, top_g, counts = _conv_out(bg, u, conv_w[ci], w_conv_out[ci].astype(BF16), x, g4,
                                                   mod_l, router_w[l].T, router_b[l][:, None], tm=tm, seq=seq,
                                                   **kw)
        x = _moe(h, top_i, top_g, counts, x, g4, mod_l, w_gu_bf, b_gate_up, w_dn_bf, b_down,
                 layer=l, tm_e=tm_e, tf=tf, tm_c=tm, **kw)

    y_prompt = x[:n_ctx].reshape(batch, seq, D)
    y_sample = x[n_ctx:].reshape(dec_batch, dec_seq, D)
    return (y_prompt, y_sample, jnp.stack(new_k, axis=1), jnp.stack(new_v, axis=1))
```
